```python
import math
import jax, jax.numpy as jnp
from jax import lax
import numpy as np

D_MODEL = 4096
BATCH = 2
SEQ = 8192
DEPTH = 1
DEC_BATCH = 8
DEC_SEQ = 64
PAST_LEN = 1024

CHUNK = 64
Q_BLOCK = 128
RET_HEADS = 8
RET_DK = 256
RET_DV = 256
MLA_HEADS = 16
NOPE_DIM = 128
ROPE_DIM = 64
V_DIM = 128
Q_LORA = 1024
KV_LORA = 512
MLA_SCALE = (NOPE_DIM + ROPE_DIM) ** -0.5
RET_QK = RET_HEADS * RET_DK
RET_VW = RET_HEADS * RET_DV
MLA_VW = MLA_HEADS * V_DIM
D_MIX = RET_VW + MLA_VW
IN_SIZES = (RET_QK, RET_QK, RET_VW, RET_VW, Q_LORA, KV_LORA, ROPE_DIM)
D_IN = 2 * RET_QK + 2 * RET_VW + Q_LORA + KV_LORA + ROPE_DIM
D_FF = 11008
ROPE_BASE = 10000.0
EPS = 1e-6

kernel_name = "hybrid_retention_mla_macaron_stream_step"

F32 = jnp.float32


def rms_norm(x, g):
    xf = x.astype(F32)
    y = xf * lax.rsqrt(jnp.mean(xf * xf, axis=-1, keepdims=True) + EPS)
    return (y * g.astype(F32)).astype(x.dtype)


def rope(x, pos):
    d = x.shape[-1]
    inv = ROPE_BASE ** (-jnp.arange(0, d, 2, dtype=F32) / d)
    ang = pos.astype(F32)[:, None] * inv[None, :]
    cos = jnp.cos(ang)[None, :, None, :]
    sin = jnp.sin(ang)[None, :, None, :]
    xf = x.astype(F32)
    x1, x2 = xf[..., : d // 2], xf[..., d // 2:]
    return jnp.concatenate([x1 * cos - x2 * sin, x2 * cos + x1 * sin], axis=-1).astype(x.dtype)


def swiglu(x, w_gate, w_up, w_down):
    return (jax.nn.silu(x @ w_gate) * (x @ w_up)) @ w_down


def retention_log_decay():
    return jnp.log1p(-jnp.exp2(-5.0 - jnp.arange(RET_HEADS, dtype=F32)))


def retention_chunk(S, q, k, v, log_g):
    L = q.shape[1]
    idx = jnp.arange(L, dtype=F32)
    diff = idx[:, None] - idx[None, :]
    dmask = jnp.where(diff[None] >= 0, jnp.exp(log_g[:, None, None] * diff[None]), 0.0)
    inner = jnp.einsum('blhd,bmhd->bhlm', q, k) * dmask[None]
    o = jnp.einsum('bhlm,bmhe->blhe', inner, v)
    q_dec = jnp.exp(log_g[None, :] * (idx[:, None] + 1.0))
    o = o + jnp.einsum('blhd,bhde->blhe', q * q_dec[None, :, :, None], S)
    k_dec = jnp.exp(log_g[None, :] * (L - 1.0 - idx)[:, None])
    S_new = jnp.exp(log_g * L)[None, :, None, None] * S + jnp.einsum(
        'blhd,blhe->bhde', k * k_dec[None, :, :, None], v)
    return S_new, o


def retention_prompt(q, k, v, log_g):
    B, S = q.shape[0], q.shape[1]
    n_c = S // CHUNK

    def to_chunks(t):
        return t.astype(F32).reshape(B, n_c, CHUNK, RET_HEADS, t.shape[-1]).transpose(1, 0, 2, 3, 4)

    S0 = jnp.zeros((B, RET_HEADS, RET_DK, RET_DV), F32)

    def step(Sc, qkv):
        qc, kc, vc = qkv
        return retention_chunk(Sc, qc, kc, vc, log_g)

    S_fin, o = lax.scan(step, S0, (to_chunks(q), to_chunks(k), to_chunks(v)))
    o = o.transpose(1, 0, 2, 3, 4).reshape(B, S, RET_HEADS, RET_DV)
    return o, S_fin


def retention_out(o, g_gate, gain):
    B, T = o.shape[0], o.shape[1]
    mu = jnp.mean(o, axis=-1, keepdims=True)
    var = jnp.mean(jnp.square(o - mu), axis=-1, keepdims=True)
    o = ((o - mu) * lax.rsqrt(var + EPS)).reshape(B, T, RET_VW) * gain.astype(F32)
    return (jax.nn.silu(g_gate.astype(F32)) * o).astype(g_gate.dtype)


def mla_scores(q_n, q_r, k_n, k_r):
    s = jnp.einsum('bqhd,bkhd->bhqk', q_n, k_n) + jnp.einsum('bqhd,bkd->bhqk', q_r, k_r)
    return s.astype(F32) * MLA_SCALE


def mla_attend_prompt(q_n, q_r, k_n, k_r, v):
    B, S = q_n.shape[0], q_n.shape[1]
    nb = S // Q_BLOCK
    qn_b = q_n.reshape(B, nb, Q_BLOCK, MLA_HEADS, NOPE_DIM).transpose(1, 0, 2, 3, 4)
    qr_b = q_r.reshape(B, nb, Q_BLOCK, MLA_HEADS, ROPE_DIM).transpose(1, 0, 2, 3, 4)
    key_chunk = jnp.arange(S) // CHUNK

    def block(args):
        qn, qr, i = args
        s = mla_scores(qn, qr, k_n, k_r)
        q_chunk = (i * Q_BLOCK + jnp.arange(Q_BLOCK)) // CHUNK
        mask = key_chunk[None, :] <= q_chunk[:, None]
        p = jax.nn.softmax(jnp.where(mask[None, None], s, -jnp.inf), axis=-1).astype(v.dtype)
        return jnp.einsum('bhqk,bkhd->bqhd', p, v)

    o = lax.map(block, (qn_b, qr_b, jnp.arange(nb)))
    return o.transpose(1, 0, 2, 3, 4).reshape(B, S, MLA_VW)


def mla_attend_full(q_n, q_r, k_n, k_r, v):
    B, T = q_n.shape[0], q_n.shape[1]
    p = jax.nn.softmax(mla_scores(q_n, q_r, k_n, k_r), axis=-1).astype(v.dtype)
    return jnp.einsum('bhqk,bkhd->bqhd', p, v).reshape(B, T, MLA_VW)


def mixer(z, pos, lw, ret_state=None, cache_c=None, cache_kr=None):
    B, T = z.shape[0], z.shape[1]
    offs = [int(v) for v in np.cumsum(IN_SIZES)[:-1]]
    rq, rk, rv, rg, q_lat, kv_lat, kr_raw = jnp.split(z, offs, axis=-1)
    q = rope(rq.reshape(B, T, RET_HEADS, RET_DK), pos)
    k = rope(rk.reshape(B, T, RET_HEADS, RET_DK), pos) * (RET_DK ** -0.5)
    v = rv.reshape(B, T, RET_HEADS, RET_DV)
    log_g = retention_log_decay()
    if ret_state is None:
        o_r, S_new = retention_prompt(q, k, v, log_g)
    else:
        S_new, o_r = retention_chunk(ret_state.astype(F32), q.astype(F32), k.astype(F32),
                                     v.astype(F32), log_g)
    ret_o = retention_out(o_r, rg, lw['g_ret'])
    qh = (rms_norm(q_lat, lw['g_qa']) @ lw['w_uq']).reshape(B, T, MLA_HEADS, NOPE_DIM + ROPE_DIM)
    q_n = rms_norm(qh[..., :NOPE_DIM], lw['g_qn'])
    q_r = rope(rms_norm(qh[..., NOPE_DIM:], lw['g_qr']), pos)
    c_new = rms_norm(kv_lat, lw['g_kva'])
    kr_new = rope(rms_norm(kr_raw, lw['g_kr'])[:, :, None, :], pos)[:, :, 0, :]
    if cache_c is None:
        c_all, kr_all = c_new, kr_new
    else:
        c_all = jnp.concatenate([cache_c.astype(c_new.dtype), c_new], axis=1)
        kr_all = jnp.concatenate([cache_kr.astype(kr_new.dtype), kr_new], axis=1)
    Tk = c_all.shape[1]
    kv = (c_all @ lw['w_ukv']).reshape(B, Tk, MLA_HEADS, NOPE_DIM + V_DIM)
    k_n = rms_norm(kv[..., :NOPE_DIM], lw['g_kn'])
    v_m = kv[..., NOPE_DIM:]
    if cache_c is None:
        mla_o = mla_attend_prompt(q_n, q_r, k_n, kr_all, v_m)
    else:
        mla_o = mla_attend_full(q_n, q_r, k_n, kr_all, v_m)
    o = jnp.concatenate([ret_o, mla_o.astype(ret_o.dtype)], axis=-1)
    return o, S_new, c_new, kr_new


def macaron_layer(x, mix, lw):
    h = x + 0.5 * swiglu(rms_norm(x, lw['g_ffn1']), lw['w1_gate'], lw['w1_up'], lw['w1_down'])
    o, s_ret, c_new, kr_new = mix(rms_norm(h, lw['g_mix']) @ lw['w_in'])
    h = h + o @ lw['w_out']
    h = h + 0.5 * swiglu(rms_norm(h, lw['g_ffn2']), lw['w2_gate'], lw['w2_up'], lw['w2_down'])
    return rms_norm(h, lw['g_final']), s_ret, c_new, kr_new


def setup_inputs(seed: int = 0) -> dict:
    key = jax.random.key(seed)
    ks = iter(jax.random.split(key, 40))

    def dense(shape, fan_in):
        return jax.random.normal(next(ks), shape, F32) * (fan_in ** -0.5)

    def gain(n):
        return 1.0 + 0.02 * jax.random.normal(next(ks), (DEPTH, n), F32)

    x_prompt = jax.random.normal(next(ks), (BATCH, SEQ, D_MODEL), F32)
    x_sample = jax.random.normal(next(ks), (DEC_BATCH, DEC_SEQ, D_MODEL), F32)
    state_ret = jax.random.normal(next(ks), (DEPTH, DEC_BATCH, RET_HEADS, RET_DK, RET_DV), F32)
    cache_ckv = jax.random.normal(next(ks), (DEPTH, DEC_BATCH, PAST_LEN, KV_LORA), F32)
    cache_krope = jax.random.normal(next(ks), (DEPTH, DEC_BATCH, PAST_LEN, ROPE_DIM), F32)
    return {
        'x_prompt': x_prompt, 'x_sample': x_sample,
        'state_ret': state_ret, 'cache_ckv': cache_ckv, 'cache_krope': cache_krope,
        'g_ffn1': gain(D_MODEL),
        'w1_gate': dense((DEPTH, D_MODEL, D_FF), D_MODEL),
        'w1_up': dense((DEPTH, D_MODEL, D_FF), D_MODEL),
        'w1_down': dense((DEPTH, D_FF, D_MODEL), D_FF),
        'g_mix': gain(D_MODEL),
        'w_in': dense((DEPTH, D_MODEL, D_IN), D_MODEL),
        'g_ret': gain(RET_VW),
        'g_qa': gain(Q_LORA),
        'w_uq': dense((DEPTH, Q_LORA, MLA_HEADS * (NOPE_DIM + ROPE_DIM)), Q_LORA),
        'g_qn': gain(NOPE_DIM),
        'g_qr': gain(ROPE_DIM),
        'g_kva': gain(KV_LORA),
        'g_kr': gain(ROPE_DIM),
        'w_ukv': dense((DEPTH, KV_LORA, MLA_HEADS * (NOPE_DIM + V_DIM)), KV_LORA),
        'g_kn': gain(NOPE_DIM),
        'w_out': dense((DEPTH, D_MIX, D_MODEL), D_MIX),
        'g_ffn2': gain(D_MODEL),
        'w2_gate': dense((DEPTH, D_MODEL, D_FF), D_MODEL),
        'w2_up': dense((DEPTH, D_MODEL, D_FF), D_MODEL),
        'w2_down': dense((DEPTH, D_FF, D_MODEL), D_FF),
        'g_final': gain(D_MODEL),
    }


def reference(x_prompt, x_sample, state_ret, cache_ckv, cache_krope,
              g_ffn1, w1_gate, w1_up, w1_down, g_mix, w_in, g_ret,
              g_qa, w_uq, g_qn, g_qr, g_kva, g_kr, w_ukv, g_kn, w_out,
              g_ffn2, w2_gate, w2_up, w2_down, g_final):
    pos_p = jnp.arange(x_prompt.shape[1])
    pos_s = PAST_LEN + jnp.arange(x_sample.shape[1])
    xp, xs = x_prompt, x_sample
    sp_list, cp_list, kp_list, ss_list, cs_list, ksl = [], [], [], [], [], []
    for l in range(DEPTH):
        lw = dict(g_ffn1=g_ffn1[l], w1_gate=w1_gate[l], w1_up=w1_up[l], w1_down=w1_down[l],
                  g_mix=g_mix[l], w_in=w_in[l], g_ret=g_ret[l], g_qa=g_qa[l], w_uq=w_uq[l],
                  g_qn=g_qn[l], g_qr=g_qr[l], g_kva=g_kva[l], g_kr=g_kr[l], w_ukv=w_ukv[l],
                  g_kn=g_kn[l], w_out=w_out[l], g_ffn2=g_ffn2[l], w2_gate=w2_gate[l],
                  w2_up=w2_up[l], w2_down=w2_down[l], g_final=g_final[l])
        xp, s_p, c_p, k_p = macaron_layer(xp, lambda z: mixer(z, pos_p, lw), lw)
        xs, s_s, c_s, k_s = macaron_layer(
            xs, lambda z: mixer(z, pos_s, lw, state_ret[l], cache_ckv[l], cache_krope[l]), lw)
        sp_list.append(s_p); cp_list.append(c_p); kp_list.append(k_p)
        ss_list.append(s_s); cs_list.append(c_s); ksl.append(k_s)
    state_ret_prompt = jnp.stack(sp_list)
    cache_ckv_prompt = jnp.stack(cp_list)
    cache_krope_prompt = jnp.stack(kp_list)
    state_ret_sample = jnp.stack(ss_list)
    cache_ckv_sample = jnp.stack(cs_list)
    cache_krope_sample = jnp.stack(ksl)
    return (xp, xs, state_ret_prompt, cache_ckv_prompt, cache_krope_prompt,
            state_ret_sample, cache_ckv_sample, cache_krope_sample)
```

```python
import functools

import jax
import jax.numpy as jnp
from jax import lax
from jax.experimental import pallas as pl
from jax.experimental.pallas import tpu as pltpu

F32 = jnp.float32
BF16 = jnp.bfloat16

EPS = 1e-6
ROPE_BASE = 10000.0
CHUNK = 64

LANES = 128
VMEM_LIMIT_BYTES = 56 * 1024 * 1024
ROW_CHUNK = 128


def _params(*sem):
    return pltpu.CompilerParams(dimension_semantics=sem, vmem_limit_bytes=VMEM_LIMIT_BYTES)


def _tile(n, pref, mult=8):
    if n <= pref:
        return n
    t = (pref // mult) * mult
    while t >= mult:
        if n % t == 0:
            return t
        t -= mult
    return n


def _rms(x, g):
    return x * lax.rsqrt(jnp.mean(x * x, axis=-1, keepdims=True) + EPS) * g


def _rmsnorm_kernel(x_ref, g_ref, o_ref):
    o_ref[...] = _rms(x_ref[...], g_ref[...]).astype(o_ref.dtype)


def _rmsnorm(x, g):
    m, d = x.shape
    tm = _tile(m, 256)
    return pl.pallas_call(
        _rmsnorm_kernel,
        grid=(m // tm,),
        in_specs=[pl.BlockSpec((tm, d), lambda i: (i, 0)),
                  pl.BlockSpec((1, d), lambda i: (0, 0))],
        out_specs=pl.BlockSpec((tm, d), lambda i: (i, 0)),
        out_shape=jax.ShapeDtypeStruct((m, d), BF16),
        compiler_params=_params("parallel"),
        name="rmsnorm",
    )(x, g.reshape(1, d))


def _ffn_kernel(x_ref, g_ref, wg_ref, wu_ref, wd_ref, gf_ref, o_ref, xn_ref, *, final_norm, tn):
    j = pl.program_id(1)
    row_chunks = [pl.ds(r, min(ROW_CHUNK, x_ref.shape[0] - r)) for r in range(0, x_ref.shape[0], ROW_CHUNK)]

    @pl.when(j == 0)
    def _():
        for rows in row_chunks:
            x = x_ref[rows, :]
            xn_ref[rows, :] = _rms(x, g_ref[...]).astype(BF16)
            o_ref[rows, :] = x

    xn = xn_ref[...]
    a = jnp.dot(xn, wg_ref[...], preferred_element_type=F32)
    b = jnp.dot(xn, wu_ref[...], preferred_element_type=F32)
    h = (0.5 * (a * (1.0 / (1.0 + jnp.exp(-a)))) * b).astype(BF16)
    d = o_ref.shape[1]
    for n in range(0, d, tn):
        o_ref[:, n:n + tn] += jnp.dot(h, wd_ref[:, n:n + tn], preferred_element_type=F32)

    if final_norm:
        @pl.when(j == pl.num_programs(1) - 1)
        def _():
            for rows in row_chunks:
                o_ref[rows, :] = _rms(o_ref[rows, :], gf_ref[...])


def _ffn(x, g, wg, wu, wd, g_final=None):
    m, d = x.shape
    f = wg.shape[1]
    tm = _tile(m, 512)
    tf = _tile(f, 256, LANES)
    final_norm = g_final is not None
    gf = (g_final if final_norm else g).reshape(1, d)
    return pl.pallas_call(
        functools.partial(_ffn_kernel, final_norm=final_norm, tn=_tile(d, 512, LANES)),
        grid=(m // tm, f // tf),
        in_specs=[pl.BlockSpec((tm, d), lambda i, j: (i, 0)),
                  pl.BlockSpec((1, d), lambda i, j: (0, 0)),
                  pl.BlockSpec((d, tf), lambda i, j: (0, j)),
                  pl.BlockSpec((d, tf), lambda i, j: (0, j)),
                  pl.BlockSpec((tf, d), lambda i, j: (j, 0)),
                  pl.BlockSpec((1, d), lambda i, j: (0, 0))],
        out_specs=pl.BlockSpec((tm, d), lambda i, j: (i, 0)),
        out_shape=jax.ShapeDtypeStruct((m, d), F32),
        scratch_shapes=[pltpu.VMEM((tm, d), BF16)],
        compiler_params=_params("parallel", "arbitrary"),
        name="ffn",
    )(x, g.reshape(1, d), wg, wu, wd, gf)


def _matmul_kernel(*refs, n_pairs, has_res):
    o_ref = refs[-1]
    acc = None
    for p in range(n_pairs):
        part = jnp.dot(refs[2 * p][...], refs[2 * p + 1][...], preferred_element_type=F32)
        acc = part if acc is None else acc + part
    if has_res:
        acc = refs[2 * n_pairs][...] + acc
    o_ref[...] = acc.astype(o_ref.dtype)


def _matmul(pairs, out_dtype, res=None, tm_pref=1024, tn_pref=512):
    m = pairs[0][0].shape[0]
    n = pairs[0][1].shape[1]
    tm = _tile(m, tm_pref)
    tn = _tile(n, tn_pref, LANES)
    in_specs, args = [], []
    for x, w in pairs:
        k = x.shape[1]
        in_specs += [pl.BlockSpec((tm, k), lambda i, j: (i, 0)),
                     pl.BlockSpec((k, tn), lambda i, j: (0, j))]
        args += [x, w]
    if res is not None:
        in_specs.append(pl.BlockSpec((tm, tn), lambda i, j: (i, j)))
        args.append(res)
    return pl.pallas_call(
        functools.partial(_matmul_kernel, n_pairs=len(pairs), has_res=res is not None),
        grid=(m // tm, n // tn),
        in_specs=in_specs,
        out_specs=pl.BlockSpec((tm, tn), lambda i, j: (i, j)),
        out_shape=jax.ShapeDtypeStruct((m, n), out_dtype),
        compiler_params=_params("parallel", "parallel"),
        name="matmul",
    )(*args)


def _retention_kernel(lg_ref, rq_ref, rk_ref, rv_ref, rg_ref, cos_ref, sin_ref, gain_ref, *rest,
                      has_state, k_scale):
    if has_state:
        s0_ref, o_ref, s_out_ref, s_scr = rest
    else:
        o_ref, s_out_ref, s_scr = rest
    c = pl.program_id(2)

    @pl.when(c == 0)
    def _():
        if has_state:
            s_scr[...] = s0_ref[0, 0]
        else:
            s_scr[...] = jnp.zeros_like(s_scr)

    lg = lg_ref[pl.program_id(1)]
    cos = cos_ref[...]
    sin = sin_ref[...]
    half = cos.shape[1]

    def rope(ref):
        x = ref[...].astype(F32)
        x1, x2 = x[:, :half], x[:, half:]
        return jnp.concatenate([x1 * cos - x2 * sin, x2 * cos + x1 * sin], axis=-1)

    q = rope(rq_ref)
    k = rope(rk_ref) * k_scale
    v = rv_ref[...]
    length = q.shape[0]
    ri = lax.broadcasted_iota(jnp.int32, (length, 1), 0).astype(F32)
    ci = lax.broadcasted_iota(jnp.int32, (1, length), 1).astype(F32)
    diff = ri - ci
    dmask = jnp.where(diff >= 0, jnp.exp(lg * diff), 0.0)
    inner = lax.dot_general(q.astype(BF16), k.astype(BF16), (((1,), (1,)), ((), ())),
                            preferred_element_type=F32) * dmask
    o = jnp.dot(inner.astype(BF16), v, preferred_element_type=F32)
    s = s_scr[...]
    q_dec = jnp.exp(lg * (ri + 1.0))
    o = o + jnp.dot((q * q_dec).astype(BF16), s.astype(BF16), preferred_element_type=F32)
    k_dec = jnp.exp(lg * (length - 1.0 - ri))
    s_new = jnp.exp(jnp.full((1, 1), lg * length, F32)) * s + lax.dot_general(
        (k * k_dec).astype(BF16), v, (((0,), (0,)), ((), ())), preferred_element_type=F32)
    s_scr[...] = s_new

    mu = jnp.mean(o, axis=-1, keepdims=True)
    oc = o - mu
    var = jnp.mean(oc * oc, axis=-1, keepdims=True)
    y = oc * lax.rsqrt(var + EPS) * gain_ref[...]
    gate = rg_ref[...].astype(F32)
    o_ref[...] = ((gate * (1.0 / (1.0 + jnp.exp(-gate)))) * y).astype(o_ref.dtype)

    @pl.when(c == pl.num_programs(2) - 1)
    def _():
        s_out_ref[0, 0] = s_new


def _retention(z, gain, log_g, cos, sin, batch, seq, heads, dk, chunk, state=None):
    assert dk == 2 * cos.shape[1] and dk % LANES == 0
    nc = seq // chunk
    has_state = state is not None

    def strip(part):
        return pl.BlockSpec((chunk, dk), lambda b, h, c, lg: (b * nc + c, part * heads + h))

    in_specs = [strip(0), strip(1), strip(2), strip(3),
                pl.BlockSpec((chunk, dk // 2), lambda b, h, c, lg: (c, 0)),
                pl.BlockSpec((chunk, dk // 2), lambda b, h, c, lg: (c, 0)),
                pl.BlockSpec((1, dk), lambda b, h, c, lg: (0, h))]
    args = [z, z, z, z, cos, sin, gain.reshape(1, heads * dk)]
    if has_state:
        in_specs.append(pl.BlockSpec((1, 1, dk, dk), lambda b, h, c, lg: (b, h, 0, 0)))
        args.append(state)
    return pl.pallas_call(
        functools.partial(_retention_kernel, has_state=has_state, k_scale=dk ** -0.5),
        grid_spec=pltpu.PrefetchScalarGridSpec(
            num_scalar_prefetch=1,
            grid=(batch, heads, nc),
            in_specs=in_specs,
            out_specs=[pl.BlockSpec((chunk, dk), lambda b, h, c, lg: (b * nc + c, h)),
                       pl.BlockSpec((1, 1, dk, dk), lambda b, h, c, lg: (b, h, 0, 0))],
            scratch_shapes=[pltpu.VMEM((dk, dk), F32)]),
        out_shape=[jax.ShapeDtypeStruct((batch * seq, heads * dk), BF16),
                   jax.ShapeDtypeStruct((batch, heads, dk, dk), F32)],
        compiler_params=_params("parallel", "parallel", "arbitrary"),
        name="retention",
    )(log_g, *args)


def _rope_padded(y, cos, sin_signed, rot):
    lane = lax.broadcasted_iota(jnp.int32, y.shape, 1)
    swapped = jnp.where(lane < rot, pltpu.roll(y, LANES - rot, 1), pltpu.roll(y, rot, 1))
    return y * cos + swapped * sin_signed


def _lat_prep_kernel(z_ref, gqa_ref, gkva_ref, gkr_ref, cos_ref, sin_ref,
                     qa_ref, c32_ref, c16_ref, kr32_ref, kr16_ref, *, q_lora, kv_lora, rope_dim):
    z = z_ref[...]
    qa_ref[...] = _rms(z[:, :q_lora], gqa_ref[...]).astype(qa_ref.dtype)
    c = _rms(z[:, q_lora:q_lora + kv_lora], gkva_ref[...])
    c32_ref[...] = c
    c16_ref[...] = c.astype(c16_ref.dtype)
    r = z[:, q_lora + kv_lora:q_lora + kv_lora + LANES]
    y = r * lax.rsqrt(jnp.sum(r * r, axis=-1, keepdims=True) * (1.0 / rope_dim) + EPS) * gkr_ref[...]
    kr = _rope_padded(y, cos_ref[...], sin_ref[...], rope_dim // 2)
    kr32_ref[...] = kr[:, :rope_dim]
    kr16_ref[...] = kr.astype(kr16_ref.dtype)


def _lat_prep(z_lat, g_qa, g_kva, g_kr_pad, cos, sin, seq, q_lora, kv_lora, rope_dim):
    m, n = z_lat.shape
    tm = _tile(seq, 256)
    nb = seq // tm
    row = lambda i: (i, 0)
    const = lambda i: (0, 0)
    pos = lambda i: (i % nb, 0)
    return pl.pallas_call(
        functools.partial(_lat_prep_kernel, q_lora=q_lora, kv_lora=kv_lora, rope_dim=rope_dim),
        grid=(m // tm,),
        in_specs=[pl.BlockSpec((tm, n), row),
                  pl.BlockSpec((1, q_lora), const),
                  pl.BlockSpec((1, kv_lora), const),
                  pl.BlockSpec((1, LANES), const),
                  pl.BlockSpec((tm, LANES), pos),
                  pl.BlockSpec((tm, LANES), pos)],
        out_specs=[pl.BlockSpec((tm, q_lora), row),
                   pl.BlockSpec((tm, kv_lora), row),
                   pl.BlockSpec((tm, kv_lora), row),
                   pl.BlockSpec((tm, rope_dim), row),
                   pl.BlockSpec((tm, LANES), row)],
        out_shape=[jax.ShapeDtypeStruct((m, q_lora), BF16),
                   jax.ShapeDtypeStruct((m, kv_lora), F32),
                   jax.ShapeDtypeStruct((m, kv_lora), BF16),
                   jax.ShapeDtypeStruct((m, rope_dim), F32),
                   jax.ShapeDtypeStruct((m, LANES), BF16)],
        compiler_params=_params("parallel"),
        name="lat_prep",
    )(z_lat, g_qa.reshape(1, -1), g_kva.reshape(1, -1), g_kr_pad, cos, sin)


def _head_prep_kernel(x_ref, gn_ref, *rest, heads, rope_dim, scale, has_rope):
    if has_rope:
        gr_ref, cos_ref, sin_ref, o_ref = rest
        cos = cos_ref[...]
        sin = sin_ref[...]
    else:
        (o_ref,) = rest
    for h in range(heads):
        lo = h * LANES
        a = x_ref[:, lo:lo + LANES]
        o_ref[:, lo:lo + LANES] = (_rms(a, gn_ref[...]) * scale).astype(o_ref.dtype)
        lo2 = (heads + h) * LANES
        r = x_ref[:, lo2:lo2 + LANES]
        if has_rope:
            y = r * lax.rsqrt(jnp.sum(r * r, axis=-1, keepdims=True) * (1.0 / rope_dim) + EPS) * gr_ref[...]
            r = _rope_padded(y, cos, sin, rope_dim // 2) * scale
        o_ref[:, lo2:lo2 + LANES] = r.astype(o_ref.dtype)


def _head_prep(x, g_norm, heads, seq=None, g_rope_pad=None, cos=None, sin=None, rope_dim=None, scale=1.0):
    m, n = x.shape
    has_rope = g_rope_pad is not None
    tm = _tile(seq if has_rope else m, 256)
    row = lambda i: (i, 0)
    const = lambda i: (0, 0)
    in_specs = [pl.BlockSpec((tm, n), row), pl.BlockSpec((1, LANES), const)]
    args = [x, g_norm.reshape(1, LANES)]
    if has_rope:
        nb = seq // tm
        pos = lambda i: (i % nb, 0)
        in_specs += [pl.BlockSpec((1, LANES), const),
                     pl.BlockSpec((tm, LANES), pos),
                     pl.BlockSpec((tm, LANES), pos)]
        args += [g_rope_pad, cos, sin]
    return pl.pallas_call(
        functools.partial(_head_prep_kernel, heads=heads, rope_dim=rope_dim, scale=scale, has_rope=has_rope),
        grid=(m // tm,),
        in_specs=in_specs,
        out_specs=pl.BlockSpec((tm, n), row),
        out_shape=jax.ShapeDtypeStruct((m, n), BF16),
        compiler_params=_params("parallel"),
        name="head_prep",
    )(*args)


def _attention_kernel(qn_ref, qr_ref, kn_ref, kr_ref, v_ref, o_ref, *, tk, causal):
    tq = qn_ref.shape[0]
    q = jnp.concatenate([qn_ref[...], qr_ref[...]], axis=-1)

    def scores(j):
        rows = pl.ds(pl.multiple_of(j * tk, tk), tk)
        kb = jnp.concatenate([kn_ref[rows, :], kr_ref[rows, :]], axis=-1)
        s = lax.dot_general(q, kb, (((1,), (1,)), ((), ())), preferred_element_type=F32)
        return s, v_ref[rows, :]

    def update(carry, s, vb):
        m, l, acc = carry
        m_new = jnp.maximum(m, jnp.max(s, axis=-1, keepdims=True))
        alpha = jnp.exp(m - m_new)
        p = jnp.exp(s - m_new)
        l = alpha * l + jnp.sum(p, axis=-1, keepdims=True)
        acc = alpha * acc + jnp.dot(p.astype(vb.dtype), vb, preferred_element_type=F32)
        return m_new, l, acc

    def body(j, carry):
        s, vb = scores(j)
        return update(carry, s, vb)

    init = (jnp.full((tq, 1), -1e30, F32), jnp.zeros((tq, 1), F32), jnp.zeros((tq, v_ref.shape[1]), F32))
    if causal:
        i = pl.program_id(2)
        carry = lax.fori_loop(0, i, body, init)
        s, vb = scores(i)
        qc = lax.broadcasted_iota(jnp.int32, (tq, tk), 0) // CHUNK
        kc = lax.broadcasted_iota(jnp.int32, (tq, tk), 1) // CHUNK
        carry = update(carry, jnp.where(kc <= qc, s, -1e30), vb)
    else:
        carry = lax.fori_loop(0, kn_ref.shape[0] // tk, body, init)
    _, l, acc = carry
    o_ref[...] = (acc / l).astype(o_ref.dtype)


def _attention(qp, kvp, kr, batch, tq_len, tk_len, heads, causal):
    if causal:
        tq = tk = _tile(tq_len, 512)
        assert tq_len == tk_len and tq % CHUNK == 0
    else:
        tq, tk = tq_len, tk_len
    nq = tq_len // tq
    return pl.pallas_call(
        functools.partial(_attention_kernel, tk=tk, causal=causal),
        grid=(batch, heads, nq),
        in_specs=[pl.BlockSpec((tq, LANES), lambda b, h, i: (b * nq + i, h)),
                  pl.BlockSpec((tq, LANES), lambda b, h, i: (b * nq + i, heads + h)),
                  pl.BlockSpec((tk_len, LANES), lambda b, h, i: (b, h)),
                  pl.BlockSpec((tk_len, LANES), lambda b, h, i: (b, 0)),
                  pl.BlockSpec((tk_len, LANES), lambda b, h, i: (b, heads + h))],
        out_specs=pl.BlockSpec((tq, LANES), lambda b, h, i: (b * nq + i, h)),
        out_shape=jax.ShapeDtypeStruct((batch * tq_len, heads * LANES), BF16),
        compiler_params=_params("parallel", "parallel", "arbitrary"),
        name="attention",
    )(qp, qp, kvp, kr, kvp)


def _rope_tables_split(pos, dim):
    inv = ROPE_BASE ** (-jnp.arange(0, dim, 2, dtype=F32) / dim)
    ang = pos.astype(F32)[:, None] * inv[None, :]
    return jnp.cos(ang), jnp.sin(ang)


def _rope_tables_padded(pos, dim):
    cos, sin = _rope_tables_split(pos, dim)
    pad = jnp.zeros((pos.shape[0], LANES - dim), F32)
    return jnp.concatenate([cos, cos, pad], axis=1), jnp.concatenate([-sin, sin, pad], axis=1)


def _pad_lanes(g):
    return jnp.concatenate([g, jnp.zeros((LANES - g.shape[0],), g.dtype)]).reshape(1, LANES)


def _prepare_weights(lw, dims):
    ret_w, q_lora, kv_lora, rope_dim, heads, nope, vdim = dims
    w_in = lw['w_in']
    d = w_in.shape[0]
    lat = w_in[:, 4 * ret_w:]
    lat_pad = (-lat.shape[1]) % (2 * LANES)
    w_uq = lw['w_uq'].reshape(q_lora, heads, nope + rope_dim)
    w_uq_r = jnp.pad(w_uq[:, :, nope:], ((0, 0), (0, 0), (0, LANES - rope_dim)))
    w_ukv = lw['w_ukv'].reshape(kv_lora, heads, nope + vdim)
    w_out = lw['w_out'].astype(BF16)
    return dict(
        w1=(lw['w1_gate'].astype(BF16), lw['w1_up'].astype(BF16), lw['w1_down'].astype(BF16)),
        w2=(lw['w2_gate'].astype(BF16), lw['w2_up'].astype(BF16), lw['w2_down'].astype(BF16)),
        w_in_main=w_in[:, :4 * ret_w].astype(BF16),
        w_in_lat=jnp.pad(lat, ((0, 0), (0, lat_pad))).astype(BF16),
        w_uq=jnp.concatenate([w_uq[:, :, :nope].reshape(q_lora, heads * nope),
                              w_uq_r.reshape(q_lora, heads * LANES)], axis=1).astype(BF16),
        w_ukv=jnp.concatenate([w_ukv[:, :, :nope].reshape(kv_lora, heads * nope),
                               w_ukv[:, :, nope:].reshape(kv_lora, heads * vdim)], axis=1).astype(BF16),
        w_out_ret=w_out[:ret_w], w_out_mla=w_out[ret_w:],
        g_qr=_pad_lanes(lw['g_qr']), g_kr=_pad_lanes(lw['g_kr']),
    )


def _layer(x, lw, pw, dims, batch, seq, pos, ret_chunk, state=None, cache_c=None, cache_kr=None):
    ret_w, q_lora, kv_lora, rope_dim, heads, nope, vdim = dims
    ret_heads, dk = lw['ret_heads'], lw['ret_dk']
    mla_scale = (nope + rope_dim) ** -0.5

    h1 = _ffn(x, lw['g_ffn1'], *pw['w1'])
    xn = _rmsnorm(h1, lw['g_mix'])
    z_main = _matmul([(xn, pw['w_in_main'])], BF16)
    z_lat = _matmul([(xn, pw['w_in_lat'])], F32, tn_pref=896)

    cos_r, sin_r = _rope_tables_split(pos, dk)
    log_g = jnp.log1p(-jnp.exp2(-5.0 - jnp.arange(ret_heads, dtype=F32)))
    ret_o, s_new = _retention(z_main, lw['g_ret'], log_g, cos_r, sin_r, batch, seq, ret_heads, dk,
                              ret_chunk, state)

    cos_m, sin_m = _rope_tables_padded(pos, rope_dim)
    qa, c_new, c_bf, kr_new, kr_bf = _lat_prep(z_lat, lw['g_qa'], lw['g_kva'], pw['g_kr'], cos_m, sin_m,
                                                seq, q_lora, kv_lora, rope_dim)
    qh = _matmul([(qa, pw['w_uq'])], F32)
    qp = _head_prep(qh, lw['g_qn'], heads, seq, pw['g_qr'], cos_m, sin_m, rope_dim, mla_scale)
    if cache_c is None:
        keys, c_all, kr_all = seq, c_bf, kr_bf
    else:
        keys = cache_c.shape[1] + seq
        c_all = jnp.concatenate([cache_c.astype(BF16), c_bf.reshape(batch, seq, kv_lora)],
                                axis=1).reshape(batch * keys, kv_lora)
        kr_pad = jnp.pad(cache_kr, ((0, 0), (0, 0), (0, LANES - rope_dim))).astype(BF16)
        kr_all = jnp.concatenate([kr_pad, kr_bf.reshape(batch, seq, LANES)], axis=1).reshape(batch * keys, LANES)
    kv = _matmul([(c_all, pw['w_ukv'])], F32)
    kvp = _head_prep(kv, lw['g_kn'], heads)
    mla_o = _attention(qp, kvp, kr_all, batch, seq, keys, heads, causal=cache_c is None)

    h2 = _matmul([(ret_o, pw['w_out_ret']), (mla_o, pw['w_out_mla'])], F32, res=h1)
    y = _ffn(h2, lw['g_ffn2'], *pw['w2'], g_final=lw['g_final'])
    return y, s_new, c_new, kr_new


def kernel(x_prompt, x_sample, state_ret, cache_ckv, cache_krope,
           g_ffn1, w1_gate, w1_up, w1_down, g_mix, w_in, g_ret,
           g_qa, w_uq, g_qn, g_qr, g_kva, g_kr, w_ukv, g_kn, w_out,
           g_ffn2, w2_gate, w2_up, w2_down, g_final):
    batch, seq, d_model = x_prompt.shape
    dec_batch, dec_seq, _ = x_sample.shape
    depth, _, ret_heads, dk, dv = state_ret.shape
    past_len, kv_lora = cache_ckv.shape[2], cache_ckv.shape[3]
    rope_dim = cache_krope.shape[3]
    q_lora = g_qa.shape[1]
    nope = g_qn.shape[1]
    heads = w_uq.shape[2] // (nope + rope_dim)
    vdim = w_ukv.shape[2] // heads - nope
    ret_w = ret_heads * dk
    assert dk == dv and nope == LANES and vdim == LANES and rope_dim <= LANES // 2
    assert w_in.shape[2] == 4 * ret_w + q_lora + kv_lora + rope_dim
    dims = (ret_w, q_lora, kv_lora, rope_dim, heads, nope, vdim)

    pos_p = jnp.arange(seq)
    pos_s = past_len + jnp.arange(dec_seq)
    xp = x_prompt.reshape(batch * seq, d_model)
    xs = x_sample.reshape(dec_batch * dec_seq, d_model)
    outs = [[] for _ in range(6)]
    for l in range(depth):
        lw = dict(g_ffn1=g_ffn1[l], w1_gate=w1_gate[l], w1_up=w1_up[l], w1_down=w1_down[l],
                  g_mix=g_mix[l], w_in=w_in[l], g_ret=g_ret[l], g_qa=g_qa[l], w_uq=w_uq[l],
                  g_qn=g_qn[l], g_qr=g_qr[l], g_kva=g_kva[l], g_kr=g_kr[l], w_ukv=w_ukv[l],
                  g_kn=g_kn[l], w_out=w_out[l], g_ffn2=g_ffn2[l], w2_gate=w2_gate[l],
                  w2_up=w2_up[l], w2_down=w2_down[l], g_final=g_final[l],
                  ret_heads=ret_heads, ret_dk=dk)
        pw = _prepare_weights(lw, dims)
        xp, s_p, c_p, k_p = _layer(xp, lw, pw, dims, batch, seq, pos_p, _tile(seq, 256))
        xs, s_s, c_s, k_s = _layer(xs, lw, pw, dims, dec_batch, dec_seq, pos_s, dec_seq,
                                   state_ret[l], cache_ckv[l], cache_krope[l])
        outs[0].append(s_p)
        outs[1].append(c_p.reshape(batch, seq, kv_lora))
        outs[2].append(k_p.reshape(batch, seq, rope_dim))
        outs[3].append(s_s)
        outs[4].append(c_s.reshape(dec_batch, dec_seq, kv_lora))
        outs[5].append(k_s.reshape(dec_batch, dec_seq, rope_dim))
    return (xp.reshape(batch, seq, d_model), xs.reshape(dec_batch, dec_seq, d_model),
            *[jnp.stack(o) for o in outs])
```

```python
import functools

import jax
import jax.numpy as jnp
from jax import lax
from jax.experimental import pallas as pl
from jax.experimental.pallas import tpu as pltpu

F32 = jnp.float32
BF16 = jnp.bfloat16

EPS = 1e-6
ROPE_BASE = 10000.0
CHUNK = 64

LANES = 128
VMEM_LIMIT_BYTES = 56 * 1024 * 1024
LAT_TN = 256
LOG2_E = 1.4426950408889634
HEAD_GROUP = 2
SOFTMAX_ROWS = 32
ROW_CHUNK = 128


def _params(*sem):
    return pltpu.CompilerParams(dimension_semantics=sem, vmem_limit_bytes=VMEM_LIMIT_BYTES)


def _tile(n, pref, mult=8):
    if n <= pref:
        return n
    t = (pref // mult) * mult
    while t >= mult:
        if n % t == 0:
            return t
        t -= mult
    return n


def _rms(x, g):
    return x * lax.rsqrt(jnp.mean(x * x, axis=-1, keepdims=True) + EPS) * g


def _rmsnorm_kernel(x_ref, g_ref, o_ref):
    o_ref[...] = _rms(x_ref[...], g_ref[...]).astype(o_ref.dtype)


def _rmsnorm(x, g):
    m, d = x.shape
    tm = _tile(m, 256)
    return pl.pallas_call(
        _rmsnorm_kernel,
        grid=(m // tm,),
        in_specs=[pl.BlockSpec((tm, d), lambda i: (i, 0)),
                  pl.BlockSpec((1, d), lambda i: (0, 0))],
        out_specs=pl.BlockSpec((tm, d), lambda i: (i, 0)),
        out_shape=jax.ShapeDtypeStruct((m, d), BF16),
        compiler_params=_params("parallel"),
        name="rmsnorm",
    )(x, g.reshape(1, d))


def _ffn_kernel(x_ref, g_ref, wg_ref, wu_ref, wd_ref, gf_ref, o_ref, xn_ref, *, final_norm, tn):
    j = pl.program_id(1)
    row_chunks = [pl.ds(r, min(ROW_CHUNK, x_ref.shape[0] - r)) for r in range(0, x_ref.shape[0], ROW_CHUNK)]

    @pl.when(j == 0)
    def _():
        for rows in row_chunks:
            x = x_ref[rows, :]
            xn_ref[rows, :] = _rms(x, g_ref[...]).astype(BF16)
            o_ref[rows, :] = x

    xn = xn_ref[...]
    a = jnp.dot(xn, wg_ref[...], preferred_element_type=F32)
    b = jnp.dot(xn, wu_ref[...], preferred_element_type=F32)
    h = (0.5 * (a * (1.0 / (1.0 + jnp.exp(-a)))) * b).astype(BF16)
    d = o_ref.shape[1]
    for n in range(0, d, tn):
        o_ref[:, n:n + tn] += jnp.dot(h, wd_ref[:, n:n + tn], preferred_element_type=F32)

    if final_norm:
        @pl.when(j == pl.num_programs(1) - 1)
        def _():
            for rows in row_chunks:
                o_ref[rows, :] = _rms(o_ref[rows, :], gf_ref[...])


def _ffn_tf(f):
    return _tile(f, 256, LANES)


def _column_blocks(w, tn):
    k, n = w.shape
    return w.reshape(k, n // tn, tn).transpose(1, 0, 2)


def _ffn(x, g, wg, wu, wd, g_final=None):
    m, d = x.shape
    f = wd.shape[0]
    tm = _tile(m, 512)
    tf = _ffn_tf(f)
    assert wg.shape == (f // tf, d, tf)
    final_norm = g_final is not None
    gf = (g_final if final_norm else g).reshape(1, d)
    return pl.pallas_call(
        functools.partial(_ffn_kernel, final_norm=final_norm, tn=_tile(d, 512, LANES)),
        grid=(m // tm, f // tf),
        in_specs=[pl.BlockSpec((tm, d), lambda i, j: (i, 0)),
                  pl.BlockSpec((1, d), lambda i, j: (0, 0)),
                  pl.BlockSpec((None, d, tf), lambda i, j: (j, 0, 0)),
                  pl.BlockSpec((None, d, tf), lambda i, j: (j, 0, 0)),
                  pl.BlockSpec((tf, d), lambda i, j: (j, 0)),
                  pl.BlockSpec((1, d), lambda i, j: (0, 0))],
        out_specs=pl.BlockSpec((tm, d), lambda i, j: (i, 0)),
        out_shape=jax.ShapeDtypeStruct((m, d), F32),
        scratch_shapes=[pltpu.VMEM((tm, d), BF16)],
        compiler_params=_params("parallel", "arbitrary"),
        name="ffn",
    )(x, g.reshape(1, d), wg, wu, wd, gf)


def _matmul_kernel(*refs, n_pairs, has_res):
    o_ref = refs[-1]
    acc = None
    for p in range(n_pairs):
        part = jnp.dot(refs[2 * p][...], refs[2 * p + 1][...], preferred_element_type=F32)
        acc = part if acc is None else acc + part
    if has_res:
        acc = refs[2 * n_pairs][...] + acc
    o_ref[...] = acc.astype(o_ref.dtype)


def _matmul(pairs, out_dtype, res=None, n=None, col0=0, tm_pref=1024, tn_pref=512):
    m = pairs[0][0].shape[0]
    n = pairs[0][1].shape[1] if n is None else n
    tm = _tile(m, tm_pref)
    tn = _tile(n, tn_pref, LANES)
    assert col0 % tn == 0
    in_specs, args = [], []
    for x, w, row0 in pairs:
        k = x.shape[1]
        assert row0 % k == 0
        in_specs += [pl.BlockSpec((tm, k), lambda i, j: (i, 0)),
                     pl.BlockSpec((k, tn), lambda i, j, rb=row0 // k: (rb, col0 // tn + j))]
        args += [x, w]
    if res is not None:
        in_specs.append(pl.BlockSpec((tm, tn), lambda i, j: (i, j)))
        args.append(res)
    return pl.pallas_call(
        functools.partial(_matmul_kernel, n_pairs=len(pairs), has_res=res is not None),
        grid=(m // tm, n // tn),
        in_specs=in_specs,
        out_specs=pl.BlockSpec((tm, tn), lambda i, j: (i, j)),
        out_shape=jax.ShapeDtypeStruct((m, n), out_dtype),
        compiler_params=_params("parallel", "parallel"),
        name="matmul",
    )(*args)


def _retention_kernel(lg_ref, rq_ref, rk_ref, rv_ref, rg_ref, cos_ref, sin_ref, gain_ref, *rest,
                      has_state, k_scale):
    if has_state:
        s0_ref, o_ref, s_out_ref, s_scr = rest
    else:
        o_ref, s_out_ref, s_scr = rest
    c = pl.program_id(2)

    @pl.when(c == 0)
    def _():
        if has_state:
            s_scr[...] = s0_ref[0, 0]
        else:
            s_scr[...] = jnp.zeros_like(s_scr)

    lg = lg_ref[pl.program_id(1)]
    cos = cos_ref[...]
    sin = sin_ref[...]
    half = cos.shape[1]

    def rope(ref):
        x = ref[...].astype(F32)
        x1, x2 = x[:, :half], x[:, half:]
        return jnp.concatenate([x1 * cos - x2 * sin, x2 * cos + x1 * sin], axis=-1)

    q = rope(rq_ref)
    k = rope(rk_ref) * k_scale
    v = rv_ref[...]
    length = q.shape[0]
    ri = lax.broadcasted_iota(jnp.int32, (length, 1), 0).astype(F32)
    ci = lax.broadcasted_iota(jnp.int32, (1, length), 1).astype(F32)
    diff = ri - ci
    dmask = jnp.where(diff >= 0, jnp.exp(lg * diff), 0.0)
    inner = lax.dot_general(q.astype(BF16), k.astype(BF16), (((1,), (1,)), ((), ())),
                            preferred_element_type=F32) * dmask
    o = jnp.dot(inner.astype(BF16), v, preferred_element_type=F32)
    s = s_scr[...]
    q_dec = jnp.exp(lg * (ri + 1.0))
    o = o + jnp.dot((q * q_dec).astype(BF16), s.astype(BF16), preferred_element_type=F32)
    k_dec = jnp.exp(lg * (length - 1.0 - ri))
    s_new = jnp.exp(jnp.full((1, 1), lg * length, F32)) * s + lax.dot_general(
        (k * k_dec).astype(BF16), v, (((0,), (0,)), ((), ())), preferred_element_type=F32)
    s_scr[...] = s_new

    mu = jnp.mean(o, axis=-1, keepdims=True)
    oc = o - mu
    var = jnp.mean(oc * oc, axis=-1, keepdims=True)
    y = oc * lax.rsqrt(var + EPS) * gain_ref[...]
    gate = rg_ref[...].astype(F32)
    o_ref[...] = ((gate * (1.0 / (1.0 + jnp.exp(-gate)))) * y).astype(o_ref.dtype)

    @pl.when(c == pl.num_programs(2) - 1)
    def _():
        s_out_ref[0, 0] = s_new


def _retention(z, gain, log_g, cos, sin, batch, seq, heads, dk, chunk, state=None):
    assert dk == 2 * cos.shape[1] and dk % LANES == 0
    nc = seq // chunk
    has_state = state is not None

    def strip(part):
        return pl.BlockSpec((chunk, dk), lambda b, h, c, lg: (b * nc + c, part * heads + h))

    in_specs = [strip(0), strip(1), strip(2), strip(3),
                pl.BlockSpec((chunk, dk // 2), lambda b, h, c, lg: (c, 0)),
                pl.BlockSpec((chunk, dk // 2), lambda b, h, c, lg: (c, 0)),
                pl.BlockSpec((1, dk), lambda b, h, c, lg: (0, h))]
    args = [z, z, z, z, cos, sin, gain.reshape(1, heads * dk)]
    if has_state:
        in_specs.append(pl.BlockSpec((1, 1, dk, dk), lambda b, h, c, lg: (b, h, 0, 0)))
        args.append(state)
    return pl.pallas_call(
        functools.partial(_retention_kernel, has_state=has_state, k_scale=dk ** -0.5),
        grid_spec=pltpu.PrefetchScalarGridSpec(
            num_scalar_prefetch=1,
            grid=(batch, heads, nc),
            in_specs=in_specs,
            out_specs=[pl.BlockSpec((chunk, dk), lambda b, h, c, lg: (b * nc + c, h)),
                       pl.BlockSpec((1, 1, dk, dk), lambda b, h, c, lg: (b, h, 0, 0))],
            scratch_shapes=[pltpu.VMEM((dk, dk), F32)]),
        out_shape=[jax.ShapeDtypeStruct((batch * seq, heads * dk), BF16),
                   jax.ShapeDtypeStruct((batch, heads, dk, dk), F32)],
        compiler_params=_params("parallel", "parallel", "arbitrary"),
        name="retention",
    )(log_g, *args)


def _rope_padded(y, cos, sin_signed, rot):
    lane = lax.broadcasted_iota(jnp.int32, y.shape, 1)
    swapped = jnp.where(lane < rot, pltpu.roll(y, LANES - rot, 1), pltpu.roll(y, rot, 1))
    return y * cos + swapped * sin_signed


def _lat_prep_kernel(z_ref, gqa_ref, gkva_ref, gkr_ref, cos_ref, sin_ref,
                     qa_ref, c32_ref, c16_ref, kr32_ref, kr16_ref, *, q_lora, kv_lora, rope_dim):
    z = z_ref[...]
    qa_ref[...] = _rms(z[:, :q_lora], gqa_ref[...]).astype(qa_ref.dtype)
    c = _rms(z[:, q_lora:q_lora + kv_lora], gkva_ref[...])
    c32_ref[...] = c
    c16_ref[...] = c.astype(c16_ref.dtype)
    r = z[:, q_lora + kv_lora:q_lora + kv_lora + LANES]
    y = r * lax.rsqrt(jnp.sum(r * r, axis=-1, keepdims=True) * (1.0 / rope_dim) + EPS) * gkr_ref[...]
    kr = _rope_padded(y, cos_ref[...], sin_ref[...], rope_dim // 2)
    kr32_ref[...] = kr[:, :rope_dim]
    kr16_ref[...] = kr.astype(kr16_ref.dtype)


def _lat_prep(z_lat, g_qa, g_kva, g_kr_pad, cos, sin, seq, q_lora, kv_lora, rope_dim):
    m, n = z_lat.shape
    tm = _tile(seq, 256)
    nb = seq // tm
    row = lambda i: (i, 0)
    const = lambda i: (0, 0)
    pos = lambda i: (i % nb, 0)
    return pl.pallas_call(
        functools.partial(_lat_prep_kernel, q_lora=q_lora, kv_lora=kv_lora, rope_dim=rope_dim),
        grid=(m // tm,),
        in_specs=[pl.BlockSpec((tm, n), row),
                  pl.BlockSpec((1, q_lora), const),
                  pl.BlockSpec((1, kv_lora), const),
                  pl.BlockSpec((1, LANES), const),
                  pl.BlockSpec((tm, LANES), pos),
                  pl.BlockSpec((tm, LANES), pos)],
        out_specs=[pl.BlockSpec((tm, q_lora), row),
                   pl.BlockSpec((tm, kv_lora), row),
                   pl.BlockSpec((tm, kv_lora), row),
                   pl.BlockSpec((tm, rope_dim), row),
                   pl.BlockSpec((tm, LANES), row)],
        out_shape=[jax.ShapeDtypeStruct((m, q_lora), BF16),
                   jax.ShapeDtypeStruct((m, kv_lora), F32),
                   jax.ShapeDtypeStruct((m, kv_lora), BF16),
                   jax.ShapeDtypeStruct((m, rope_dim), F32),
                   jax.ShapeDtypeStruct((m, LANES), BF16)],
        compiler_params=_params("parallel"),
        name="lat_prep",
    )(z_lat, g_qa.reshape(1, -1), g_kva.reshape(1, -1), g_kr_pad, cos, sin)


def _head_matmul_kernel(x_ref, w_ref, gn_ref, *rest, norm_blocks, rope_dim, scale, has_rope):
    if has_rope:
        gr_ref, cos_ref, sin_ref, o_ref = rest
    else:
        (o_ref,) = rest
    acc = jnp.dot(x_ref[...], w_ref[...], preferred_element_type=F32)
    j = pl.program_id(1)
    groups = [slice(lo, lo + LANES) for lo in range(0, o_ref.shape[1], LANES)]

    @pl.when(j < norm_blocks)
    def _():
        for g in groups:
            o_ref[:, g] = (_rms(acc[:, g], gn_ref[...]) * scale).astype(o_ref.dtype)

    @pl.when(j >= norm_blocks)
    def _():
        for g in groups:
            r = acc[:, g]
            if has_rope:
                y = r * lax.rsqrt(jnp.sum(r * r, axis=-1, keepdims=True) * (1.0 / rope_dim) + EPS) * gr_ref[...]
                r = _rope_padded(y, cos_ref[...], sin_ref[...], rope_dim // 2) * scale
            o_ref[:, g] = r.astype(o_ref.dtype)


def _head_matmul(x, w, g_norm, seq=None, g_rope_pad=None, cos=None, sin=None, rope_dim=None, scale=1.0):
    m, k = x.shape
    n = w.shape[1]
    has_rope = g_rope_pad is not None
    tm = _tile(m, 1024)
    tn = _tile(n // 2, 512, LANES)
    const = lambda i, j: (0, 0)
    in_specs = [pl.BlockSpec((tm, k), lambda i, j: (i, 0)),
                pl.BlockSpec((k, tn), lambda i, j: (0, j)),
                pl.BlockSpec((1, LANES), const)]
    args = [x, w, g_norm.reshape(1, LANES)]
    if has_rope:
        if tm >= seq:
            assert tm % seq == 0
            cos, sin = jnp.tile(cos, (tm // seq, 1)), jnp.tile(sin, (tm // seq, 1))
            nb = 1
        else:
            assert seq % tm == 0
            nb = seq // tm
        pos = lambda i, j: (i % nb, 0)
        in_specs += [pl.BlockSpec((1, LANES), const),
                     pl.BlockSpec((tm, LANES), pos),
                     pl.BlockSpec((tm, LANES), pos)]
        args += [g_rope_pad, cos, sin]
    return pl.pallas_call(
        functools.partial(_head_matmul_kernel, norm_blocks=n // 2 // tn, rope_dim=rope_dim, scale=scale,
                          has_rope=has_rope),
        grid=(m // tm, n // tn),
        in_specs=in_specs,
        out_specs=pl.BlockSpec((tm, tn), lambda i, j: (i, j)),
        out_shape=jax.ShapeDtypeStruct((m, n), BF16),
        compiler_params=_params("parallel", "parallel"),
        name="head_matmul",
    )(*args)


def _attention_kernel(qn_ref, qr_ref, kn_ref, kr_ref, v_ref, o_ref,
                      s_scr, p_scr, m_scr, l_scr, a_scr, acc_scr, *, tk, causal):
    tq = qn_ref.shape[0]
    heads = [slice(g * LANES, (g + 1) * LANES) for g in range(qn_ref.shape[1] // LANES)]
    q = [jnp.concatenate([qn_ref[:, g], qr_ref[:, g]], axis=-1) for g in heads]
    chunks = [slice(r, min(r + SOFTMAX_ROWS, tq)) for r in range(0, tq, SOFTMAX_ROWS)]
    m_scr[...] = jnp.full(m_scr.shape, -1e30, F32)
    l_scr[...] = jnp.zeros(l_scr.shape, F32)
    acc_scr[...] = jnp.zeros(acc_scr.shape, F32)

    def key_rows(j):
        return pl.ds(pl.multiple_of(j * tk, tk), tk)

    def scores(j):
        rows = key_rows(j)
        kr = kr_ref[rows, :]
        for n, g in enumerate(heads):
            kb = jnp.concatenate([kn_ref[rows, g], kr], axis=-1)
            s_scr[n] = lax.dot_general(q[n], kb, (((1,), (1,)), ((), ())), preferred_element_type=F32)

    def softmax_pv(j, key0=None):
        rows = key_rows(j)
        for n, g in enumerate(heads):
            for r in chunks:
                s = s_scr[n, r, :]
                if key0 is not None:
                    qc = (r.start + lax.broadcasted_iota(jnp.int32, s.shape, 0)) // CHUNK
                    kc = (key0 + lax.broadcasted_iota(jnp.int32, s.shape, 1)) // CHUNK
                    s = jnp.where(kc <= qc, s, -1e30)
                m_old = m_scr[n, r, :]
                m_new = jnp.maximum(m_old, jnp.max(s, axis=-1, keepdims=True))
                alpha = jnp.exp2(m_old - m_new)
                if tk % LANES == 0:
                    p = jnp.exp2(s - pltpu.repeat(m_new, tk // LANES, axis=1))
                else:
                    p = jnp.exp2(s - m_new[:, :1])
                p_scr[n, r, :] = p.astype(p_scr.dtype)
                m_scr[n, r, :] = m_new
                l_scr[n, r, :] = alpha * l_scr[n, r, :] + jnp.sum(p, axis=-1, keepdims=True)
                a_scr[n, r, :] = alpha
            pv = jnp.dot(p_scr[n], v_ref[rows, g], preferred_element_type=F32)
            for r in chunks:
                acc_scr[n, r, :] = a_scr[n, r, :] * acc_scr[n, r, :] + pv[r]

    def body(j, carry):
        scores(j)
        softmax_pv(j)
        return carry

    if causal:
        first_diag = pl.program_id(2) * (tq // tk)
        lax.fori_loop(0, first_diag, body, 0)
        for d in range(tq // tk):
            scores(first_diag + d)
            softmax_pv(first_diag + d, key0=d * tk)
    else:
        lax.fori_loop(0, kn_ref.shape[0] // tk, body, 0)
    for n, g in enumerate(heads):
        o_ref[:, g] = (acc_scr[n] / l_scr[n]).astype(o_ref.dtype)


def _attention(qp, kvp, kr, batch, tq_len, tk_len, heads, causal):
    if causal:
        tk = _tile(tk_len, 512)
        tq = _tile(tq_len, 1024, tk)
        assert tq_len == tk_len and tk % CHUNK == 0 and tq % tk == 0
    else:
        tq, tk = tq_len, tk_len
    nq = tq_len // tq
    assert heads % HEAD_GROUP == 0
    hg = heads // HEAD_GROUP
    wg = HEAD_GROUP * LANES
    return pl.pallas_call(
        functools.partial(_attention_kernel, tk=tk, causal=causal),
        grid=(batch, hg, nq),
        in_specs=[pl.BlockSpec((tq, wg), lambda b, h, i: (b * nq + i, h)),
                  pl.BlockSpec((tq, wg), lambda b, h, i: (b * nq + i, hg + h)),
                  pl.BlockSpec((tk_len, wg), lambda b, h, i: (b, h)),
                  pl.BlockSpec((tk_len, LANES), lambda b, h, i: (b, 0)),
                  pl.BlockSpec((tk_len, wg), lambda b, h, i: (b, hg + h))],
        out_specs=pl.BlockSpec((tq, wg), lambda b, h, i: (b * nq + i, h)),
        out_shape=jax.ShapeDtypeStruct((batch * tq_len, heads * LANES), BF16),
        scratch_shapes=[pltpu.VMEM((HEAD_GROUP, tq, tk), F32),
                        pltpu.VMEM((HEAD_GROUP, tq, tk), BF16),
                        pltpu.VMEM((HEAD_GROUP, tq, LANES), F32),
                        pltpu.VMEM((HEAD_GROUP, tq, LANES), F32),
                        pltpu.VMEM((HEAD_GROUP, tq, LANES), F32),
                        pltpu.VMEM((HEAD_GROUP, tq, LANES), F32)],
        compiler_params=_params("parallel", "parallel", "arbitrary"),
        name="attention",
    )(qp, qp, kvp, kr, kvp)


def _rope_tables_split(pos, dim):
    inv = ROPE_BASE ** (-jnp.arange(0, dim, 2, dtype=F32) / dim)
    ang = pos.astype(F32)[:, None] * inv[None, :]
    return jnp.cos(ang), jnp.sin(ang)


def _rope_tables_padded(pos, dim):
    cos, sin = _rope_tables_split(pos, dim)
    pad = jnp.zeros((pos.shape[0], LANES - dim), F32)
    return jnp.concatenate([cos, cos, pad], axis=1), jnp.concatenate([-sin, sin, pad], axis=1)


def _pad_lanes(g):
    return jnp.concatenate([g, jnp.zeros((LANES - g.shape[0],), g.dtype)]).reshape(1, LANES)


def _prepare_weights(lw, dims):
    ret_w, q_lora, kv_lora, rope_dim, heads, nope, vdim = dims
    w_in = lw['w_in']
    tf = _ffn_tf(lw['w1_gate'].shape[1])
    w_uq = lw['w_uq'].reshape(q_lora, heads, nope + rope_dim)
    w_uq_r = jnp.pad(w_uq[:, :, nope:], ((0, 0), (0, 0), (0, LANES - rope_dim)))
    w_ukv = lw['w_ukv'].reshape(kv_lora, heads, nope + vdim)
    return dict(
        w1=(_column_blocks(lw['w1_gate'].astype(BF16), tf), _column_blocks(lw['w1_up'].astype(BF16), tf),
            lw['w1_down'].astype(BF16)),
        w2=(_column_blocks(lw['w2_gate'].astype(BF16), tf), _column_blocks(lw['w2_up'].astype(BF16), tf),
            lw['w2_down'].astype(BF16)),
        w_in=jnp.pad(w_in, ((0, 0), (0, (-w_in.shape[1]) % LAT_TN))).astype(BF16),
        w_uq=jnp.concatenate([w_uq[:, :, :nope].reshape(q_lora, heads * nope),
                              w_uq_r.reshape(q_lora, heads * LANES)], axis=1).astype(BF16),
        w_ukv=jnp.concatenate([w_ukv[:, :, :nope].reshape(kv_lora, heads * nope),
                               w_ukv[:, :, nope:].reshape(kv_lora, heads * vdim)], axis=1).astype(BF16),
        w_out=lw['w_out'].astype(BF16),
        g_qr=_pad_lanes(lw['g_qr']), g_kr=_pad_lanes(lw['g_kr']),
    )


def _layer(x, lw, pw, dims, batch, seq, pos, ret_chunk, state=None, cache_c=None, cache_kr=None):
    ret_w, q_lora, kv_lora, rope_dim, heads, nope, vdim = dims
    ret_heads, dk = lw['ret_heads'], lw['ret_dk']
    q_scale = (nope + rope_dim) ** -0.5 * LOG2_E

    h1 = _ffn(x, lw['g_ffn1'], *pw['w1'])
    xn = _rmsnorm(h1, lw['g_mix'])
    z_main = _matmul([(xn, pw['w_in'], 0)], BF16, n=4 * ret_w)
    z_lat = _matmul([(xn, pw['w_in'], 0)], F32, n=pw['w_in'].shape[1] - 4 * ret_w, col0=4 * ret_w,
                    tn_pref=LAT_TN)

    cos_r, sin_r = _rope_tables_split(pos, dk)
    log_g = jnp.log1p(-jnp.exp2(-5.0 - jnp.arange(ret_heads, dtype=F32)))
    ret_o, s_new = _retention(z_main, lw['g_ret'], log_g, cos_r, sin_r, batch, seq, ret_heads, dk,
                              ret_chunk, state)

    cos_m, sin_m = _rope_tables_padded(pos, rope_dim)
    qa, c_new, c_bf, kr_new, kr_bf = _lat_prep(z_lat, lw['g_qa'], lw['g_kva'], pw['g_kr'], cos_m, sin_m,
                                                seq, q_lora, kv_lora, rope_dim)
    qp = _head_matmul(qa, pw['w_uq'], lw['g_qn'], seq, pw['g_qr'], cos_m, sin_m, rope_dim, q_scale)
    if cache_c is None:
        keys, c_all, kr_all = seq, c_bf, kr_bf
    else:
        keys = cache_c.shape[1] + seq
        c_all = jnp.concatenate([cache_c.astype(BF16), c_bf.reshape(batch, seq, kv_lora)],
                                axis=1).reshape(batch * keys, kv_lora)
        kr_pad = jnp.pad(cache_kr, ((0, 0), (0, 0), (0, LANES - rope_dim))).astype(BF16)
        kr_all = jnp.concatenate([kr_pad, kr_bf.reshape(batch, seq, LANES)], axis=1).reshape(batch * keys, LANES)
    kvp = _head_matmul(c_all, pw['w_ukv'], lw['g_kn'])
    mla_o = _attention(qp, kvp, kr_all, batch, seq, keys, heads, causal=cache_c is None)

    h2 = _matmul([(ret_o, pw['w_out'], 0), (mla_o, pw['w_out'], ret_w)], F32, res=h1)
    y = _ffn(h2, lw['g_ffn2'], *pw['w2'], g_final=lw['g_final'])
    return y, s_new, c_new, kr_new


def kernel(x_prompt, x_sample, state_ret, cache_ckv, cache_krope,
           g_ffn1, w1_gate, w1_up, w1_down, g_mix, w_in, g_ret,
           g_qa, w_uq, g_qn, g_qr, g_kva, g_kr, w_ukv, g_kn, w_out,
           g_ffn2, w2_gate, w2_up, w2_down, g_final):
    batch, seq, d_model = x_prompt.shape
    dec_batch, dec_seq, _ = x_sample.shape
    depth, _, ret_heads, dk, dv = state_ret.shape
    past_len, kv_lora = cache_ckv.shape[2], cache_ckv.shape[3]
    rope_dim = cache_krope.shape[3]
    q_lora = g_qa.shape[1]
    nope = g_qn.shape[1]
    heads = w_uq.shape[2] // (nope + rope_dim)
    vdim = w_ukv.shape[2] // heads - nope
    ret_w = ret_heads * dk
    assert dk == dv and nope == LANES and vdim == LANES and rope_dim <= LANES // 2
    assert w_in.shape[2] == 4 * ret_w + q_lora + kv_lora + rope_dim
    dims = (ret_w, q_lora, kv_lora, rope_dim, heads, nope, vdim)

    pos_p = jnp.arange(seq)
    pos_s = past_len + jnp.arange(dec_seq)
    xp = x_prompt.reshape(batch * seq, d_model)
    xs = x_sample.reshape(dec_batch * dec_seq, d_model)
    outs = [[] for _ in range(6)]
    for l in range(depth):
        lw = dict(g_ffn1=g_ffn1[l], w1_gate=w1_gate[l], w1_up=w1_up[l], w1_down=w1_down[l],
                  g_mix=g_mix[l], w_in=w_in[l], g_ret=g_ret[l], g_qa=g_qa[l], w_uq=w_uq[l],
                  g_qn=g_qn[l], g_qr=g_qr[l], g_kva=g_kva[l], g_kr=g_kr[l], w_ukv=w_ukv[l],
                  g_kn=g_kn[l], w_out=w_out[l], g_ffn2=g_ffn2[l], w2_gate=w2_gate[l],
                  w2_up=w2_up[l], w2_down=w2_down[l], g_final=g_final[l],
                  ret_heads=ret_heads, ret_dk=dk)
        pw = _prepare_weights(lw, dims)
        xp, s_p, c_p, k_p = _layer(xp, lw, pw, dims, batch, seq, pos_p, _tile(seq, 256))
        xs, s_s, c_s, k_s = _layer(xs, lw, pw, dims, dec_batch, dec_seq, pos_s, dec_seq,
                                   state_ret[l], cache_ckv[l], cache_krope[l])
        outs[0].append(s_p)
        outs[1].append(c_p.reshape(batch, seq, kv_lora))
        outs[2].append(k_p.reshape(batch, seq, rope_dim))
        outs[3].append(s_s)
        outs[4].append(c_s.reshape(dec_batch, dec_seq, kv_lora))
        outs[5].append(k_s.reshape(dec_batch, dec_seq, rope_dim))
    return (xp.reshape(batch, seq, d_model), xs.reshape(dec_batch, dec_seq, d_model),
            *[jnp.stack(o) for o in outs])
```

```python
import functools

import jax
import jax.numpy as jnp
from jax import lax
from jax.experimental import pallas as pl
from jax.experimental.pallas import tpu as pltpu

F32 = jnp.float32
BF16 = jnp.bfloat16

EPS = 1e-6
ROPE_BASE = 10000.0
CHUNK = 64

LANES = 128
VMEM_LIMIT_BYTES = 56 * 1024 * 1024
LAT_TN = 256
LOG2_E = 1.4426950408889634
HEAD_GROUP = 2
HEAD_ROWS = 256
SOFTMAX_ROWS = 32
FFN_ROWS = 512
ROW_CHUNK = 128


def _params(*sem):
    return pltpu.CompilerParams(dimension_semantics=sem, vmem_limit_bytes=VMEM_LIMIT_BYTES)


def _tile(n, pref, mult=8):
    if n <= pref:
        return n
    t = (pref // mult) * mult
    while t >= mult:
        if n % t == 0:
            return t
        t -= mult
    return n


def _rms(x, g):
    return x * lax.rsqrt(jnp.mean(x * x, axis=-1, keepdims=True) + EPS) * g


def _rmsnorm_kernel(x_ref, g_ref, o_ref):
    o_ref[...] = _rms(x_ref[...], g_ref[...]).astype(o_ref.dtype)


def _rmsnorm(x, g):
    m, d = x.shape
    tm = _tile(m, 256)
    return pl.pallas_call(
        _rmsnorm_kernel,
        grid=(m // tm,),
        in_specs=[pl.BlockSpec((tm, d), lambda i: (i, 0)),
                  pl.BlockSpec((1, d), lambda i: (0, 0))],
        out_specs=pl.BlockSpec((tm, d), lambda i: (i, 0)),
        out_shape=jax.ShapeDtypeStruct((m, d), BF16),
        compiler_params=_params("parallel"),
        name="rmsnorm",
    )(x, g.reshape(1, d))


def _ffn_kernel(x_hbm, g_ref, wg_ref, wu_ref, wd_ref, gf_ref, o_ref, xn_ref, x_sems, *, final_norm, tn):
    i = pl.program_id(0)
    j = pl.program_id(1)
    tm, d = o_ref.shape
    row_chunks = [(r, min(ROW_CHUNK, tm - r)) for r in range(0, tm, ROW_CHUNK)]

    def x_copy(k):
        r, n = row_chunks[k]
        return pltpu.make_async_copy(x_hbm.at[pl.ds(i * tm + r, n), :], o_ref.at[pl.ds(r, n), :], x_sems.at[k])

    @pl.when(j == 0)
    def _():
        for k in range(len(row_chunks)):
            x_copy(k).start()
        for k, (r, n) in enumerate(row_chunks):
            x_copy(k).wait()
            xn_ref[r:r + n, :] = _rms(o_ref[r:r + n, :], g_ref[...]).astype(BF16)

    for r0 in range(0, tm, FFN_ROWS):
        rows = slice(r0, min(r0 + FFN_ROWS, tm))
        xn = xn_ref[rows, :]
        a = jnp.dot(xn, wg_ref[...], preferred_element_type=F32)
        b = jnp.dot(xn, wu_ref[...], preferred_element_type=F32)
        h = (0.5 * (a * (1.0 / (1.0 + jnp.exp(-a)))) * b).astype(BF16)
        for n in range(0, d, tn):
            o_ref[rows, n:n + tn] += jnp.dot(h, wd_ref[:, n:n + tn], preferred_element_type=F32)

    if final_norm:
        @pl.when(j == pl.num_programs(1) - 1)
        def _():
            for r, n in row_chunks:
                o_ref[r:r + n, :] = _rms(o_ref[r:r + n, :], gf_ref[...])


def _ffn(x, g, wg, wu, wd, g_final=None):
    m, d = x.shape
    f = wg.shape[1]
    tm = _tile(m, 1024)
    tf = _tile(f, 256, LANES)
    final_norm = g_final is not None
    gf = (g_final if final_norm else g).reshape(1, d)
    return pl.pallas_call(
        functools.partial(_ffn_kernel, final_norm=final_norm, tn=_tile(d, 512, LANES)),
        grid=(m // tm, f // tf),
        in_specs=[pl.BlockSpec(memory_space=pl.ANY),
                  pl.BlockSpec((1, d), lambda i, j: (0, 0)),
                  pl.BlockSpec((d, tf), lambda i, j: (0, j)),
                  pl.BlockSpec((d, tf), lambda i, j: (0, j)),
                  pl.BlockSpec((tf, d), lambda i, j: (j, 0)),
                  pl.BlockSpec((1, d), lambda i, j: (0, 0))],
        out_specs=pl.BlockSpec((tm, d), lambda i, j: (i, 0)),
        out_shape=jax.ShapeDtypeStruct((m, d), F32),
        scratch_shapes=[pltpu.VMEM((tm, d), BF16),
                        pltpu.SemaphoreType.DMA((-(-tm // ROW_CHUNK),))],
        compiler_params=_params("parallel", "arbitrary"),
        name="ffn",
    )(x, g.reshape(1, d), wg, wu, wd, gf)


def _matmul_kernel(*refs, n_pairs, has_res, n_valid):
    o_ref = refs[-1]
    acc = None
    for p in range(n_pairs):
        part = jnp.dot(refs[2 * p][...], refs[2 * p + 1][...], preferred_element_type=F32)
        acc = part if acc is None else acc + part
    if has_res:
        acc = refs[2 * n_pairs][...] + acc
    if n_valid is not None:
        col = pl.program_id(1) * o_ref.shape[1] + lax.broadcasted_iota(jnp.int32, acc.shape, 1)
        acc = jnp.where(col < n_valid, acc, 0.0)
    o_ref[...] = acc.astype(o_ref.dtype)


def _matmul(pairs, out_dtype, res=None, n=None, col0=0, tm_pref=1024, tn_pref=512):
    m = pairs[0][0].shape[0]
    n = pairs[0][1].shape[1] if n is None else n
    tm = _tile(m, tm_pref)
    tn = _tile(n, tn_pref, LANES)
    assert col0 % tn == 0
    in_specs, args = [], []
    for x, w, row0 in pairs:
        k = x.shape[1]
        assert row0 % k == 0
        in_specs += [pl.BlockSpec((tm, k), lambda i, j: (i, 0)),
                     pl.BlockSpec((k, tn), lambda i, j, rb=row0 // k: (rb, col0 // tn + j))]
        args += [x, w]
    if res is not None:
        in_specs.append(pl.BlockSpec((tm, tn), lambda i, j: (i, j)))
        args.append(res)
    w_cols = pairs[0][1].shape[1] - col0
    return pl.pallas_call(
        functools.partial(_matmul_kernel, n_pairs=len(pairs), has_res=res is not None,
                          n_valid=w_cols if w_cols < n else None),
        grid=(m // tm, n // tn),
        in_specs=in_specs,
        out_specs=pl.BlockSpec((tm, tn), lambda i, j: (i, j)),
        out_shape=jax.ShapeDtypeStruct((m, n), out_dtype),
        compiler_params=_params("parallel", "parallel"),
        name="matmul",
    )(*args)


def _retention_kernel(lg_ref, rq_ref, rk_ref, rv_ref, rg_ref, cos_ref, sin_ref, gain_ref, *rest,
                      has_state, k_scale):
    if has_state:
        s0_ref, o_ref, s_out_ref, s_scr, dmask_scr = rest
    else:
        o_ref, s_out_ref, s_scr, dmask_scr = rest
    c = pl.program_id(2)
    lg = lg_ref[pl.program_id(1)]
    length = o_ref.shape[0]
    ri = lax.broadcasted_iota(jnp.int32, (length, 1), 0).astype(F32)

    @pl.when(c == 0)
    def _():
        if has_state:
            s_scr[...] = s0_ref[0, 0]
        else:
            s_scr[...] = jnp.zeros_like(s_scr)
        diff = ri - lax.broadcasted_iota(jnp.int32, (1, length), 1).astype(F32)
        dmask_scr[...] = jnp.where(diff >= 0, jnp.exp(lg * diff), 0.0)

    cos = cos_ref[...]
    sin = sin_ref[...]
    half = cos.shape[1]

    def rope(ref):
        x = ref[...].astype(F32)
        x1, x2 = x[:, :half], x[:, half:]
        return jnp.concatenate([x1 * cos - x2 * sin, x2 * cos + x1 * sin], axis=-1)

    q = rope(rq_ref)
    k = rope(rk_ref) * k_scale
    v = rv_ref[...]
    inner = lax.dot_general(q.astype(BF16), k.astype(BF16), (((1,), (1,)), ((), ())),
                            preferred_element_type=F32) * dmask_scr[...]
    o = jnp.dot(inner.astype(BF16), v, preferred_element_type=F32)
    s = s_scr[...]
    q_dec = jnp.exp(lg * (ri + 1.0))
    o = o + jnp.dot((q * q_dec).astype(BF16), s.astype(BF16), preferred_element_type=F32)
    k_dec = jnp.exp(lg * (length - 1.0 - ri))
    s_new = jnp.exp(jnp.full((1, 1), lg * length, F32)) * s + lax.dot_general(
        (k * k_dec).astype(BF16), v, (((0,), (0,)), ((), ())), preferred_element_type=F32)
    s_scr[...] = s_new

    mu = jnp.mean(o, axis=-1, keepdims=True)
    oc = o - mu
    var = jnp.mean(oc * oc, axis=-1, keepdims=True)
    y = oc * lax.rsqrt(var + EPS) * gain_ref[...]
    gate = rg_ref[...].astype(F32)
    o_ref[...] = ((gate * (1.0 / (1.0 + jnp.exp(-gate)))) * y).astype(o_ref.dtype)

    @pl.when(c == pl.num_programs(2) - 1)
    def _():
        s_out_ref[0, 0] = s_new


def _retention(z, gain, log_g, cos, sin, batch, seq, heads, dk, chunk, state=None):
    assert dk == 2 * cos.shape[1] and dk % LANES == 0
    nc = seq // chunk
    has_state = state is not None

    def strip(part):
        return pl.BlockSpec((chunk, dk), lambda b, h, c, lg: (b * nc + c, part * heads + h))

    in_specs = [strip(0), strip(1), strip(2), strip(3),
                pl.BlockSpec((chunk, dk // 2), lambda b, h, c, lg: (c, 0)),
                pl.BlockSpec((chunk, dk // 2), lambda b, h, c, lg: (c, 0)),
                pl.BlockSpec((1, dk), lambda b, h, c, lg: (0, h))]
    args = [z, z, z, z, cos, sin, gain.reshape(1, heads * dk)]
    if has_state:
        in_specs.append(pl.BlockSpec((1, 1, dk, dk), lambda b, h, c, lg: (b, h, 0, 0)))
        args.append(state)
    return pl.pallas_call(
        functools.partial(_retention_kernel, has_state=has_state, k_scale=dk ** -0.5),
        grid_spec=pltpu.PrefetchScalarGridSpec(
            num_scalar_prefetch=1,
            grid=(batch, heads, nc),
            in_specs=in_specs,
            out_specs=[pl.BlockSpec((chunk, dk), lambda b, h, c, lg: (b * nc + c, h)),
                       pl.BlockSpec((1, 1, dk, dk), lambda b, h, c, lg: (b, h, 0, 0))],
            scratch_shapes=[pltpu.VMEM((dk, dk), F32), pltpu.VMEM((chunk, chunk), F32)]),
        out_shape=[jax.ShapeDtypeStruct((batch * seq, heads * dk), BF16),
                   jax.ShapeDtypeStruct((batch, heads, dk, dk), F32)],
        compiler_params=_params("parallel", "parallel", "arbitrary"),
        name="retention",
    )(log_g, *args)


def _rope_padded(y, cos, sin_signed, rot):
    lane = lax.broadcasted_iota(jnp.int32, y.shape, 1)
    swapped = jnp.where(lane < rot, pltpu.roll(y, LANES - rot, 1), pltpu.roll(y, rot, 1))
    return y * cos + swapped * sin_signed


def _lat_prep_kernel(z_ref, gqa_ref, gkva_ref, gkr_ref, cos_ref, sin_ref,
                     qa_ref, c32_ref, c16_ref, kr32_ref, kr16_ref, *, q_lora, kv_lora, rope_dim):
    z = z_ref[...]
    qa_ref[...] = _rms(z[:, :q_lora], gqa_ref[...]).astype(qa_ref.dtype)
    c = _rms(z[:, q_lora:q_lora + kv_lora], gkva_ref[...])
    c32_ref[...] = c
    c16_ref[...] = c.astype(c16_ref.dtype)
    r = z[:, q_lora + kv_lora:q_lora + kv_lora + LANES]
    y = r * lax.rsqrt(jnp.sum(r * r, axis=-1, keepdims=True) * (1.0 / rope_dim) + EPS) * gkr_ref[...]
    kr = _rope_padded(y, cos_ref[...], sin_ref[...], rope_dim // 2)
    kr32_ref[...] = kr[:, :rope_dim]
    kr16_ref[...] = kr.astype(kr16_ref.dtype)


def _lat_prep(z_lat, g_qa, g_kva, g_kr_pad, cos, sin, seq, q_lora, kv_lora, rope_dim):
    m, n = z_lat.shape
    tm = _tile(seq, 256)
    nb = seq // tm
    row = lambda i: (i, 0)
    const = lambda i: (0, 0)
    pos = lambda i: (i % nb, 0)
    return pl.pallas_call(
        functools.partial(_lat_prep_kernel, q_lora=q_lora, kv_lora=kv_lora, rope_dim=rope_dim),
        grid=(m // tm,),
        in_specs=[pl.BlockSpec((tm, n), row),
                  pl.BlockSpec((1, q_lora), const),
                  pl.BlockSpec((1, kv_lora), const),
                  pl.BlockSpec((1, LANES), const),
                  pl.BlockSpec((tm, LANES), pos),
                  pl.BlockSpec((tm, LANES), pos)],
        out_specs=[pl.BlockSpec((tm, q_lora), row),
                   pl.BlockSpec((tm, kv_lora), row),
                   pl.BlockSpec((tm, kv_lora), row),
                   pl.BlockSpec((tm, rope_dim), row),
                   pl.BlockSpec((tm, LANES), row)],
        out_shape=[jax.ShapeDtypeStruct((m, q_lora), BF16),
                   jax.ShapeDtypeStruct((m, kv_lora), F32),
                   jax.ShapeDtypeStruct((m, kv_lora), BF16),
                   jax.ShapeDtypeStruct((m, rope_dim), F32),
                   jax.ShapeDtypeStruct((m, LANES), BF16)],
        compiler_params=_params("parallel"),
        name="lat_prep",
    )(z_lat, g_qa.reshape(1, -1), g_kva.reshape(1, -1), g_kr_pad, cos, sin)


def _head_matmul_kernel(x_ref, w_ref, g_ref, *rest, rope_dim, scale):
    o_ref = rest[-1]
    groups = [slice(lo, lo + LANES) for lo in range(0, o_ref.shape[1], LANES)]
    for r0 in range(0, o_ref.shape[0], HEAD_ROWS):
        rows = slice(r0, min(r0 + HEAD_ROWS, o_ref.shape[0]))
        acc = jnp.dot(x_ref[rows, :], w_ref[...], preferred_element_type=F32)
        for g in groups:
            a = acc[:, g]
            if rope_dim is None:
                y = _rms(a, g_ref[...])
            else:
                cos_ref, sin_ref = rest[:2]
                y = a * lax.rsqrt(jnp.sum(a * a, axis=-1, keepdims=True) * (1.0 / rope_dim) + EPS) * g_ref[...]
                y = _rope_padded(y, cos_ref[rows, :], sin_ref[rows, :], rope_dim // 2)
            o_ref[rows, g] = (y * scale).astype(o_ref.dtype)


def _head_matmul(x, w, col0, n, g, scale=1.0, seq=None, cos=None, sin=None, rope_dim=None):
    m, k = x.shape
    tm = _tile(m, 1024)
    tn = _tile(n, 512, LANES)
    assert col0 % tn == 0
    const = lambda i, j: (0, 0)
    in_specs = [pl.BlockSpec((tm, k), lambda i, j: (i, 0)),
                pl.BlockSpec((k, tn), lambda i, j: (0, col0 // tn + j)),
                pl.BlockSpec((1, LANES), const)]
    args = [x, w, g.reshape(1, LANES)]
    if rope_dim is not None:
        if tm >= seq:
            assert tm % seq == 0
            cos, sin = jnp.tile(cos, (tm // seq, 1)), jnp.tile(sin, (tm // seq, 1))
            nb = 1
        else:
            assert seq % tm == 0
            nb = seq // tm
        pos = lambda i, j: (i % nb, 0)
        in_specs += [pl.BlockSpec((tm, LANES), pos), pl.BlockSpec((tm, LANES), pos)]
        args += [cos, sin]
    return pl.pallas_call(
        functools.partial(_head_matmul_kernel, rope_dim=rope_dim, scale=scale),
        grid=(m // tm, n // tn),
        in_specs=in_specs,
        out_specs=pl.BlockSpec((tm, tn), lambda i, j: (i, j)),
        out_shape=jax.ShapeDtypeStruct((m, n), BF16),
        compiler_params=_params("parallel", "parallel"),
        name="head_matmul",
    )(*args)


def _attention_kernel(qn_ref, qr_ref, kn_ref, kr_ref, v_ref, o_ref,
                      s_scr, p_scr, m_scr, l_scr, a_scr, acc_scr, *, tk, causal):
    tq = qn_ref.shape[0]
    heads = [slice(g * LANES, (g + 1) * LANES) for g in range(qn_ref.shape[1] // LANES)]
    q = [jnp.concatenate([qn_ref[:, g], qr_ref[:, g]], axis=-1) for g in heads]
    chunks = [slice(r, min(r + SOFTMAX_ROWS, tq)) for r in range(0, tq, SOFTMAX_ROWS)]
    m_scr[...] = jnp.full(m_scr.shape, -1e30, F32)
    l_scr[...] = jnp.zeros(l_scr.shape, F32)
    acc_scr[...] = jnp.zeros(acc_scr.shape, F32)

    def key_rows(j):
        return pl.ds(pl.multiple_of(j * tk, tk), tk)

    def scores(j):
        rows = key_rows(j)
        kr = kr_ref[rows, :]
        for n, g in enumerate(heads):
            kb = jnp.concatenate([kn_ref[rows, g], kr], axis=-1)
            s_scr[n] = lax.dot_general(q[n], kb, (((1,), (1,)), ((), ())), preferred_element_type=F32)

    def softmax_pv(j, key0=None):
        rows = key_rows(j)
        for n, g in enumerate(heads):
            for r in chunks:
                s = s_scr[n, r, :]
                if key0 is not None:
                    qc = (r.start + lax.broadcasted_iota(jnp.int32, s.shape, 0)) // CHUNK
                    kc = (key0 + lax.broadcasted_iota(jnp.int32, s.shape, 1)) // CHUNK
                    s = jnp.where(kc <= qc, s, -1e30)
                m_old = m_scr[n, r, :]
                m_new = jnp.maximum(m_old, jnp.max(s, axis=-1, keepdims=True))
                alpha = jnp.exp2(m_old - m_new)
                if tk % LANES == 0:
                    p = jnp.exp2(s - jnp.concatenate([m_new] * (tk // LANES), axis=1))
                else:
                    p = jnp.exp2(s - m_new[:, :1])
                p_scr[n, r, :] = p.astype(p_scr.dtype)
                m_scr[n, r, :] = m_new
                l_scr[n, r, :] = alpha * l_scr[n, r, :] + jnp.sum(p, axis=-1, keepdims=True)
                a_scr[n, r, :] = alpha
            pv = jnp.dot(p_scr[n], v_ref[rows, g], preferred_element_type=F32)
            for r in chunks:
                acc_scr[n, r, :] = a_scr[n, r, :] * acc_scr[n, r, :] + pv[r]

    def body(j, carry):
        scores(j)
        softmax_pv(j)
        return carry

    if causal:
        first_diag = pl.program_id(2) * (tq // tk)
        lax.fori_loop(0, first_diag, body, 0)
        for d in range(tq // tk):
            scores(first_diag + d)
            softmax_pv(first_diag + d, key0=d * tk)
    else:
        lax.fori_loop(0, kn_ref.shape[0] // tk, body, 0)
    for n, g in enumerate(heads):
        o_ref[:, g] = (acc_scr[n] / l_scr[n]).astype(o_ref.dtype)


def _attention(qn, qr, kn, kr, v, batch, tq_len, tk_len, heads, causal):
    if causal:
        tk = _tile(tk_len, 512)
        tq = _tile(tq_len, 1024, tk)
        assert tq_len == tk_len and tk % CHUNK == 0 and tq % tk == 0
    else:
        tq, tk = tq_len, tk_len
    nq = tq_len // tq
    assert heads % HEAD_GROUP == 0
    hg = heads // HEAD_GROUP
    wg = HEAD_GROUP * LANES
    return pl.pallas_call(
        functools.partial(_attention_kernel, tk=tk, causal=causal),
        grid=(batch, hg, nq),
        in_specs=[pl.BlockSpec((tq, wg), lambda b, h, i: (b * nq + i, h)),
                  pl.BlockSpec((tq, wg), lambda b, h, i: (b * nq + i, h)),
                  pl.BlockSpec((tk_len, wg), lambda b, h, i: (b, h)),
                  pl.BlockSpec((tk_len, LANES), lambda b, h, i: (b, 0)),
                  pl.BlockSpec((tk_len, wg), lambda b, h, i: (b, h))],
        out_specs=pl.BlockSpec((tq, wg), lambda b, h, i: (b * nq + i, h)),
        out_shape=jax.ShapeDtypeStruct((batch * tq_len, heads * LANES), BF16),
        scratch_shapes=[pltpu.VMEM((HEAD_GROUP, tq, tk), F32),
                        pltpu.VMEM((HEAD_GROUP, tq, tk), BF16),
                        pltpu.VMEM((HEAD_GROUP, tq, LANES), F32),
                        pltpu.VMEM((HEAD_GROUP, tq, LANES), F32),
                        pltpu.VMEM((HEAD_GROUP, tq, LANES), F32),
                        pltpu.VMEM((HEAD_GROUP, tq, LANES), F32)],
        compiler_params=_params("parallel", "parallel", "arbitrary"),
        name="attention",
    )(qn, qr, kn, kr, v)


def _rope_tables_split(pos, dim):
    inv = ROPE_BASE ** (-jnp.arange(0, dim, 2, dtype=F32) / dim)
    ang = pos.astype(F32)[:, None] * inv[None, :]
    return jnp.cos(ang), jnp.sin(ang)


def _rope_tables_padded(pos, dim):
    cos, sin = _rope_tables_split(pos, dim)
    pad = jnp.zeros((pos.shape[0], LANES - dim), F32)
    return jnp.concatenate([cos, cos, pad], axis=1), jnp.concatenate([-sin, sin, pad], axis=1)


def _pad_lanes(g):
    return jnp.concatenate([g, jnp.zeros((LANES - g.shape[0],), g.dtype)]).reshape(1, LANES)


def _prepare_weights(lw, dims):
    ret_w, q_lora, kv_lora, rope_dim, heads, nope, vdim = dims
    w_in = lw['w_in']
    w_uq = lw['w_uq'].reshape(q_lora, heads, nope + rope_dim)
    w_uq_r = jnp.pad(w_uq[:, :, nope:], ((0, 0), (0, 0), (0, LANES - rope_dim)))
    w_ukv = lw['w_ukv'].reshape(kv_lora, heads, nope + vdim)
    return dict(
        w1=(lw['w1_gate'].astype(BF16), lw['w1_up'].astype(BF16), lw['w1_down'].astype(BF16)),
        w2=(lw['w2_gate'].astype(BF16), lw['w2_up'].astype(BF16), lw['w2_down'].astype(BF16)),
        w_in=w_in.astype(BF16),
        w_uq=jnp.concatenate([w_uq[:, :, :nope].reshape(q_lora, heads * nope),
                              w_uq_r.reshape(q_lora, heads * LANES)], axis=1).astype(BF16),
        w_ukv=jnp.concatenate([w_ukv[:, :, :nope].reshape(kv_lora, heads * nope),
                               w_ukv[:, :, nope:].reshape(kv_lora, heads * vdim)], axis=1).astype(BF16),
        w_out=lw['w_out'].astype(BF16),
        g_qr=_pad_lanes(lw['g_qr']), g_kr=_pad_lanes(lw['g_kr']),
    )


def _layer(x, lw, pw, dims, batch, seq, pos, ret_chunk, state=None, cache_c=None, cache_kr=None):
    ret_w, q_lora, kv_lora, rope_dim, heads, nope, vdim = dims
    ret_heads, dk = lw['ret_heads'], lw['ret_dk']
    q_scale = (nope + rope_dim) ** -0.5 * LOG2_E

    h1 = _ffn(x, lw['g_ffn1'], *pw['w1'])
    xn = _rmsnorm(h1, lw['g_mix'])
    z_main = _matmul([(xn, pw['w_in'], 0)], BF16, n=4 * ret_w)
    n_lat = q_lora + kv_lora + LANES
    z_lat = _matmul([(xn, pw['w_in'], 0)], F32, n=n_lat + (-n_lat) % LAT_TN, col0=4 * ret_w, tn_pref=LAT_TN)

    cos_r, sin_r = _rope_tables_split(pos, dk)
    log_g = jnp.log1p(-jnp.exp2(-5.0 - jnp.arange(ret_heads, dtype=F32)))
    ret_o, s_new = _retention(z_main, lw['g_ret'], log_g, cos_r, sin_r, batch, seq, ret_heads, dk,
                              ret_chunk, state)

    cos_m, sin_m = _rope_tables_padded(pos, rope_dim)
    qa, c_new, c_bf, kr_new, kr_bf = _lat_prep(z_lat, lw['g_qa'], lw['g_kva'], pw['g_kr'], cos_m, sin_m,
                                                seq, q_lora, kv_lora, rope_dim)
    hw = heads * LANES
    qn = _head_matmul(qa, pw['w_uq'], 0, hw, lw['g_qn'], q_scale)
    qr = _head_matmul(qa, pw['w_uq'], hw, hw, pw['g_qr'], q_scale, seq, cos_m, sin_m, rope_dim)
    if cache_c is None:
        keys, c_all, kr_all = seq, c_bf, kr_bf
    else:
        keys = cache_c.shape[1] + seq
        c_all = jnp.concatenate([cache_c.astype(BF16), c_bf.reshape(batch, seq, kv_lora)],
                                axis=1).reshape(batch * keys, kv_lora)
        kr_pad = jnp.pad(cache_kr, ((0, 0), (0, 0), (0, LANES - rope_dim))).astype(BF16)
        kr_all = jnp.concatenate([kr_pad, kr_bf.reshape(batch, seq, LANES)], axis=1).reshape(batch * keys, LANES)
    kn = _head_matmul(c_all, pw['w_ukv'], 0, hw, lw['g_kn'])
    v = _matmul([(c_all, pw['w_ukv'], 0)], BF16, n=hw, col0=hw)
    mla_o = _attention(qn, qr, kn, kr_all, v, batch, seq, keys, heads, causal=cache_c is None)

    h2 = _matmul([(ret_o, pw['w_out'], 0), (mla_o, pw['w_out'], ret_w)], F32, res=h1)
    y = _ffn(h2, lw['g_ffn2'], *pw['w2'], g_final=lw['g_final'])
    return y, s_new, c_new, kr_new


def kernel(x_prompt, x_sample, state_ret, cache_ckv, cache_krope,
           g_ffn1, w1_gate, w1_up, w1_down, g_mix, w_in, g_ret,
           g_qa, w_uq, g_qn, g_qr, g_kva, g_kr, w_ukv, g_kn, w_out,
           g_ffn2, w2_gate, w2_up, w2_down, g_final):
    batch, seq, d_model = x_prompt.shape
    dec_batch, dec_seq, _ = x_sample.shape
    depth, _, ret_heads, dk, dv = state_ret.shape
    past_len, kv_lora = cache_ckv.shape[2], cache_ckv.shape[3]
    rope_dim = cache_krope.shape[3]
    q_lora = g_qa.shape[1]
    nope = g_qn.shape[1]
    heads = w_uq.shape[2] // (nope + rope_dim)
    vdim = w_ukv.shape[2] // heads - nope
    ret_w = ret_heads * dk
    assert dk == dv and nope == LANES and vdim == LANES and rope_dim <= LANES // 2
    assert w_in.shape[2] == 4 * ret_w + q_lora + kv_lora + rope_dim
    dims = (ret_w, q_lora, kv_lora, rope_dim, heads, nope, vdim)

    pos_p = jnp.arange(seq)
    pos_s = past_len + jnp.arange(dec_seq)
    xp = x_prompt.reshape(batch * seq, d_model)
    xs = x_sample.reshape(dec_batch * dec_seq, d_model)
    outs = [[] for _ in range(6)]
    for l in range(depth):
        lw = dict(g_ffn1=g_ffn1[l], w1_gate=w1_gate[l], w1_up=w1_up[l], w1_down=w1_down[l],
                  g_mix=g_mix[l], w_in=w_in[l], g_ret=g_ret[l], g_qa=g_qa[l], w_uq=w_uq[l],
                  g_qn=g_qn[l], g_qr=g_qr[l], g_kva=g_kva[l], g_kr=g_kr[l], w_ukv=w_ukv[l],
                  g_kn=g_kn[l], w_out=w_out[l], g_ffn2=g_ffn2[l], w2_gate=w2_gate[l],
                  w2_up=w2_up[l], w2_down=w2_down[l], g_final=g_final[l],
                  ret_heads=ret_heads, ret_dk=dk)
        pw = _prepare_weights(lw, dims)
        xp, s_p, c_p, k_p = _layer(xp, lw, pw, dims, batch, seq, pos_p, _tile(seq, 256))
        xs, s_s, c_s, k_s = _layer(xs, lw, pw, dims, dec_batch, dec_seq, pos_s, dec_seq,
                                   state_ret[l], cache_ckv[l], cache_krope[l])
        outs[0].append(s_p)
        outs[1].append(c_p.reshape(batch, seq, kv_lora))
        outs[2].append(k_p.reshape(batch, seq, rope_dim))
        outs[3].append(s_s)
        outs[4].append(c_s.reshape(dec_batch, dec_seq, kv_lora))
        outs[5].append(k_s.reshape(dec_batch, dec_seq, rope_dim))
    return (xp.reshape(batch, seq, d_model), xs.reshape(dec_batch, dec_seq, d_model),
            *[jnp.stack(o) for o in outs])
```

```python
import functools

import jax
import jax.numpy as jnp
from jax import lax
from jax.experimental import pallas as pl
from jax.experimental.pallas import tpu as pltpu

F32 = jnp.float32
BF16 = jnp.bfloat16

EPS = 1e-6
ROPE_BASE = 10000.0
CHUNK = 64

LANES = 128
VMEM_LIMIT_BYTES = 56 * 1024 * 1024
LAT_TN = 256
LOG2_E = 1.4426950408889634
HEAD_GROUP = 2
RET_GROUP = 2
HEAD_ROWS = 256
SOFTMAX_ROWS = 32
FFN_ROWS = 512
ROW_CHUNK = 128


def _params(*sem):
    return pltpu.CompilerParams(dimension_semantics=sem, vmem_limit_bytes=VMEM_LIMIT_BYTES)


def _tile(n, pref, mult=8):
    if n <= pref:
        return n
    t = (pref // mult) * mult
    while t >= mult:
        if n % t == 0:
            return t
        t -= mult
    return n


def _rms(x, g):
    return x * lax.rsqrt(jnp.mean(x * x, axis=-1, keepdims=True) + EPS) * g


def _rmsnorm_kernel(x_ref, g_ref, o_ref):
    o_ref[...] = _rms(x_ref[...], g_ref[...]).astype(o_ref.dtype)


def _rmsnorm(x, g):
    m, d = x.shape
    tm = _tile(m, 256)
    return pl.pallas_call(
        _rmsnorm_kernel,
        grid=(m // tm,),
        in_specs=[pl.BlockSpec((tm, d), lambda i: (i, 0)),
                  pl.BlockSpec((1, d), lambda i: (0, 0))],
        out_specs=pl.BlockSpec((tm, d), lambda i: (i, 0)),
        out_shape=jax.ShapeDtypeStruct((m, d), BF16),
        compiler_params=_params("parallel"),
        name="rmsnorm",
    )(x, g.reshape(1, d))


def _ffn_kernel(x_hbm, g_ref, wg_ref, wu_ref, wd_ref, gf_ref, o_ref, xn_ref, x_sems, *, final_norm, tn):
    i = pl.program_id(0)
    j = pl.program_id(1)
    tm, d = o_ref.shape
    row_chunks = [(r, min(ROW_CHUNK, tm - r)) for r in range(0, tm, ROW_CHUNK)]

    def x_copy(k):
        r, n = row_chunks[k]
        return pltpu.make_async_copy(x_hbm.at[pl.ds(i * tm + r, n), :], o_ref.at[pl.ds(r, n), :], x_sems.at[k])

    @pl.when(j == 0)
    def _():
        for k in range(len(row_chunks)):
            x_copy(k).start()
        for k, (r, n) in enumerate(row_chunks):
            x_copy(k).wait()
            xn_ref[r:r + n, :] = _rms(o_ref[r:r + n, :], g_ref[...]).astype(BF16)

    for r0 in range(0, tm, FFN_ROWS):
        rows = slice(r0, min(r0 + FFN_ROWS, tm))
        xn = xn_ref[rows, :]
        a = jnp.dot(xn, wg_ref[...], preferred_element_type=F32)
        b = jnp.dot(xn, wu_ref[...], preferred_element_type=F32)
        h = (0.5 * (a * (1.0 / (1.0 + jnp.exp(-a)))) * b).astype(BF16)
        for n in range(0, d, tn):
            o_ref[rows, n:n + tn] += jnp.dot(h, wd_ref[:, n:n + tn], preferred_element_type=F32)

    if final_norm:
        @pl.when(j == pl.num_programs(1) - 1)
        def _():
            for r, n in row_chunks:
                o_ref[r:r + n, :] = _rms(o_ref[r:r + n, :], gf_ref[...])


def _ffn(x, g, wg, wu, wd, g_final=None):
    m, d = x.shape
    f = wg.shape[1]
    tm = _tile(m, 1024)
    tf = _tile(f, 256, LANES)
    final_norm = g_final is not None
    gf = (g_final if final_norm else g).reshape(1, d)
    return pl.pallas_call(
        functools.partial(_ffn_kernel, final_norm=final_norm, tn=_tile(d, 512, LANES)),
        grid=(m // tm, f // tf),
        in_specs=[pl.BlockSpec(memory_space=pl.ANY),
                  pl.BlockSpec((1, d), lambda i, j: (0, 0)),
                  pl.BlockSpec((d, tf), lambda i, j: (0, j)),
                  pl.BlockSpec((d, tf), lambda i, j: (0, j)),
                  pl.BlockSpec((tf, d), lambda i, j: (j, 0)),
                  pl.BlockSpec((1, d), lambda i, j: (0, 0))],
        out_specs=pl.BlockSpec((tm, d), lambda i, j: (i, 0)),
        out_shape=jax.ShapeDtypeStruct((m, d), F32),
        scratch_shapes=[pltpu.VMEM((tm, d), BF16),
                        pltpu.SemaphoreType.DMA((-(-tm // ROW_CHUNK),))],
        compiler_params=_params("parallel", "arbitrary"),
        name="ffn",
    )(x, g.reshape(1, d), wg, wu, wd, gf)


def _matmul_kernel(*refs, n_pairs, has_res, n_valid):
    o_ref = refs[-1]
    acc = None
    for p in range(n_pairs):
        part = jnp.dot(refs[2 * p][...], refs[2 * p + 1][...], preferred_element_type=F32)
        acc = part if acc is None else acc + part
    if has_res:
        acc = refs[2 * n_pairs][...] + acc
    if n_valid is not None:
        col = pl.program_id(1) * o_ref.shape[1] + lax.broadcasted_iota(jnp.int32, acc.shape, 1)
        acc = jnp.where(col < n_valid, acc, 0.0)
    o_ref[...] = acc.astype(o_ref.dtype)


def _matmul(pairs, out_dtype, res=None, n=None, col0=0, tm_pref=1024, tn_pref=512):
    m = pairs[0][0].shape[0]
    n = pairs[0][1].shape[1] if n is None else n
    tm = _tile(m, tm_pref)
    tn = _tile(n, tn_pref, LANES)
    assert col0 % tn == 0
    in_specs, args = [], []
    for x, w, row0 in pairs:
        k = x.shape[1]
        assert row0 % k == 0
        in_specs += [pl.BlockSpec((tm, k), lambda i, j: (i, 0)),
                     pl.BlockSpec((k, tn), lambda i, j, rb=row0 // k: (rb, col0 // tn + j))]
        args += [x, w]
    if res is not None:
        in_specs.append(pl.BlockSpec((tm, tn), lambda i, j: (i, j)))
        args.append(res)
    w_cols = pairs[0][1].shape[1] - col0
    return pl.pallas_call(
        functools.partial(_matmul_kernel, n_pairs=len(pairs), has_res=res is not None,
                          n_valid=w_cols if w_cols < n else None),
        grid=(m // tm, n // tn),
        in_specs=in_specs,
        out_specs=pl.BlockSpec((tm, tn), lambda i, j: (i, j)),
        out_shape=jax.ShapeDtypeStruct((m, n), out_dtype),
        compiler_params=_params("parallel", "parallel"),
        name="matmul",
    )(*args)


def _retention_kernel(lg_ref, rq_ref, rk_ref, rv_ref, rg_ref, cos_ref, sin_ref, gain_ref, *rest,
                      has_state, k_scale):
    if has_state:
        s0_ref, o_ref, s_out_ref, s_scr, dmask_scr = rest
    else:
        o_ref, s_out_ref, s_scr, dmask_scr = rest
    c = pl.program_id(2)
    length = o_ref.shape[0]
    group = s_scr.shape[0]
    dk = s_scr.shape[1]
    ri = lax.broadcasted_iota(jnp.int32, (length, 1), 0).astype(F32)
    log_g = [lg_ref[pl.program_id(1) * group + n] for n in range(group)]

    @pl.when(c == 0)
    def _():
        diff = ri - lax.broadcasted_iota(jnp.int32, (1, length), 1).astype(F32)
        for n in range(group):
            if has_state:
                s_scr[n] = s0_ref[0, n]
            else:
                s_scr[n] = jnp.zeros((dk, dk), F32)
            dmask_scr[n] = jnp.where(diff >= 0, jnp.exp(log_g[n] * diff), 0.0)

    pos = pl.ds(pl.multiple_of(c * length, length), length)
    cos = cos_ref[pos, :]
    sin = sin_ref[pos, :]
    half = cos.shape[1]

    def rope(x):
        x = x.astype(F32)
        x1, x2 = x[:, :half], x[:, half:]
        return jnp.concatenate([x1 * cos - x2 * sin, x2 * cos + x1 * sin], axis=-1)

    for n in range(group):
        cols = slice(n * dk, (n + 1) * dk)
        lg = log_g[n]
        q = rope(rq_ref[:, cols])
        k = rope(rk_ref[:, cols]) * k_scale
        v = rv_ref[:, cols]
        inner = lax.dot_general(q.astype(BF16), k.astype(BF16), (((1,), (1,)), ((), ())),
                                preferred_element_type=F32) * dmask_scr[n]
        o = jnp.dot(inner.astype(BF16), v, preferred_element_type=F32)
        s = s_scr[n]
        q_dec = jnp.exp(lg * (ri + 1.0))
        o = o + jnp.dot((q * q_dec).astype(BF16), s.astype(BF16), preferred_element_type=F32)
        k_dec = jnp.exp(lg * (length - 1.0 - ri))
        s_new = jnp.exp(jnp.full((1, 1), lg * length, F32)) * s + lax.dot_general(
            (k * k_dec).astype(BF16), v, (((0,), (0,)), ((), ())), preferred_element_type=F32)
        s_scr[n] = s_new

        mu = jnp.mean(o, axis=-1, keepdims=True)
        oc = o - mu
        var = jnp.mean(oc * oc, axis=-1, keepdims=True)
        y = oc * lax.rsqrt(var + EPS) * gain_ref[:, cols]
        gate = rg_ref[:, cols].astype(F32)
        o_ref[:, cols] = ((gate * (1.0 / (1.0 + jnp.exp(-gate)))) * y).astype(o_ref.dtype)

    @pl.when(c == pl.num_programs(2) - 1)
    def _():
        s_out_ref[0] = s_scr[...]


def _retention(z, gain, log_g, cos, sin, batch, seq, heads, dk, chunk, state=None):
    assert dk == 2 * cos.shape[1] and dk % LANES == 0 and heads % RET_GROUP == 0
    nc = seq // chunk
    has_state = state is not None
    hg = heads // RET_GROUP
    wg = RET_GROUP * dk

    def strip(part):
        return pl.BlockSpec((chunk, wg), lambda b, h, c, lg: (b * nc + c, part * hg + h))

    in_specs = [strip(0), strip(1), strip(2), strip(3),
                pl.BlockSpec((seq, dk // 2), lambda b, h, c, lg: (0, 0)),
                pl.BlockSpec((seq, dk // 2), lambda b, h, c, lg: (0, 0)),
                pl.BlockSpec((1, wg), lambda b, h, c, lg: (0, h))]
    args = [z, z, z, z, cos, sin, gain.reshape(1, heads * dk)]
    if has_state:
        in_specs.append(pl.BlockSpec((1, RET_GROUP, dk, dk), lambda b, h, c, lg: (b, h, 0, 0)))
        args.append(state)
    return pl.pallas_call(
        functools.partial(_retention_kernel, has_state=has_state, k_scale=dk ** -0.5),
        grid_spec=pltpu.PrefetchScalarGridSpec(
            num_scalar_prefetch=1,
            grid=(batch, hg, nc),
            in_specs=in_specs,
            out_specs=[pl.BlockSpec((chunk, wg), lambda b, h, c, lg: (b * nc + c, h)),
                       pl.BlockSpec((1, RET_GROUP, dk, dk), lambda b, h, c, lg: (b, h, 0, 0))],
            scratch_shapes=[pltpu.VMEM((RET_GROUP, dk, dk), F32), pltpu.VMEM((RET_GROUP, chunk, chunk), F32)]),
        out_shape=[jax.ShapeDtypeStruct((batch * seq, heads * dk), BF16),
                   jax.ShapeDtypeStruct((batch, heads, dk, dk), F32)],
        compiler_params=_params("parallel", "parallel", "arbitrary"),
        name="retention",
    )(log_g, *args)


def _rope_padded(y, cos, sin_signed, rot):
    lane = lax.broadcasted_iota(jnp.int32, y.shape, 1)
    swapped = jnp.where(lane < rot, pltpu.roll(y, LANES - rot, 1), pltpu.roll(y, rot, 1))
    return y * cos + swapped * sin_signed


def _lat_prep_kernel(z_ref, gqa_ref, gkva_ref, gkr_ref, cos_ref, sin_ref,
                     qa_ref, c32_ref, c16_ref, kr32_ref, kr16_ref, *, q_lora, kv_lora, rope_dim):
    z = z_ref[...]
    qa_ref[...] = _rms(z[:, :q_lora], gqa_ref[...]).astype(qa_ref.dtype)
    c = _rms(z[:, q_lora:q_lora + kv_lora], gkva_ref[...])
    c32_ref[...] = c
    c16_ref[...] = c.astype(c16_ref.dtype)
    r = z[:, q_lora + kv_lora:q_lora + kv_lora + LANES]
    y = r * lax.rsqrt(jnp.sum(r * r, axis=-1, keepdims=True) * (1.0 / rope_dim) + EPS) * gkr_ref[...]
    kr = _rope_padded(y, cos_ref[...], sin_ref[...], rope_dim // 2)
    kr32_ref[...] = kr[:, :rope_dim]
    kr16_ref[...] = kr.astype(kr16_ref.dtype)


def _lat_prep(z_lat, g_qa, g_kva, g_kr_pad, cos, sin, seq, q_lora, kv_lora, rope_dim):
    m, n = z_lat.shape
    tm = _tile(seq, 256)
    nb = seq // tm
    row = lambda i: (i, 0)
    const = lambda i: (0, 0)
    pos = lambda i: (i % nb, 0)
    return pl.pallas_call(
        functools.partial(_lat_prep_kernel, q_lora=q_lora, kv_lora=kv_lora, rope_dim=rope_dim),
        grid=(m // tm,),
        in_specs=[pl.BlockSpec((tm, n), row),
                  pl.BlockSpec((1, q_lora), const),
                  pl.BlockSpec((1, kv_lora), const),
                  pl.BlockSpec((1, LANES), const),
                  pl.BlockSpec((tm, LANES), pos),
                  pl.BlockSpec((tm, LANES), pos)],
        out_specs=[pl.BlockSpec((tm, q_lora), row),
                   pl.BlockSpec((tm, kv_lora), row),
                   pl.BlockSpec((tm, kv_lora), row),
                   pl.BlockSpec((tm, rope_dim), row),
                   pl.BlockSpec((tm, LANES), row)],
        out_shape=[jax.ShapeDtypeStruct((m, q_lora), BF16),
                   jax.ShapeDtypeStruct((m, kv_lora), F32),
                   jax.ShapeDtypeStruct((m, kv_lora), BF16),
                   jax.ShapeDtypeStruct((m, rope_dim), F32),
                   jax.ShapeDtypeStruct((m, LANES), BF16)],
        compiler_params=_params("parallel"),
        name="lat_prep",
    )(z_lat, g_qa.reshape(1, -1), g_kva.reshape(1, -1), g_kr_pad, cos, sin)


def _head_matmul_kernel(x_ref, w_ref, g_ref, *rest, rope_dim, scale):
    o_ref = rest[-1]
    groups = [slice(lo, lo + LANES) for lo in range(0, o_ref.shape[1], LANES)]
    for r0 in range(0, o_ref.shape[0], HEAD_ROWS):
        rows = slice(r0, min(r0 + HEAD_ROWS, o_ref.shape[0]))
        acc = jnp.dot(x_ref[rows, :], w_ref[...], preferred_element_type=F32)
        for g in groups:
            a = acc[:, g]
            if rope_dim is None:
                y = _rms(a, g_ref[...])
            else:
                cos_ref, sin_ref = rest[:2]
                y = a * lax.rsqrt(jnp.sum(a * a, axis=-1, keepdims=True) * (1.0 / rope_dim) + EPS) * g_ref[...]
                y = _rope_padded(y, cos_ref[rows, :], sin_ref[rows, :], rope_dim // 2)
            o_ref[rows, g] = (y * scale).astype(o_ref.dtype)


def _head_matmul(x, w, col0, n, g, scale=1.0, seq=None, cos=None, sin=None, rope_dim=None):
    m, k = x.shape
    tm = _tile(m, 1024)
    tn = _tile(n, 512, LANES)
    assert col0 % tn == 0
    const = lambda i, j: (0, 0)
    in_specs = [pl.BlockSpec((tm, k), lambda i, j: (i, 0)),
                pl.BlockSpec((k, tn), lambda i, j: (0, col0 // tn + j)),
                pl.BlockSpec((1, LANES), const)]
    args = [x, w, g.reshape(1, LANES)]
    if rope_dim is not None:
        if tm >= seq:
            assert tm % seq == 0
            cos, sin = jnp.tile(cos, (tm // seq, 1)), jnp.tile(sin, (tm // seq, 1))
            nb = 1
        else:
            assert seq % tm == 0
            nb = seq // tm
        pos = lambda i, j: (i % nb, 0)
        in_specs += [pl.BlockSpec((tm, LANES), pos), pl.BlockSpec((tm, LANES), pos)]
        args += [cos, sin]
    return pl.pallas_call(
        functools.partial(_head_matmul_kernel, rope_dim=rope_dim, scale=scale),
        grid=(m // tm, n // tn),
        in_specs=in_specs,
        out_specs=pl.BlockSpec((tm, tn), lambda i, j: (i, j)),
        out_shape=jax.ShapeDtypeStruct((m, n), BF16),
        compiler_params=_params("parallel", "parallel"),
        name="head_matmul",
    )(*args)


def _attention_kernel(qn_ref, qr_ref, kn_ref, kr_ref, v_ref, o_ref,
                      s_scr, p_scr, m_scr, l_scr, a_scr, acc_scr, *, tk, causal):
    tq = qn_ref.shape[0]
    heads = [slice(g * LANES, (g + 1) * LANES) for g in range(qn_ref.shape[1] // LANES)]
    q = [jnp.concatenate([qn_ref[:, g], qr_ref[:, g]], axis=-1) for g in heads]
    chunks = [slice(r, min(r + SOFTMAX_ROWS, tq)) for r in range(0, tq, SOFTMAX_ROWS)]
    m_scr[...] = jnp.full(m_scr.shape, -1e30, F32)
    l_scr[...] = jnp.zeros(l_scr.shape, F32)
    acc_scr[...] = jnp.zeros(acc_scr.shape, F32)

    def key_rows(j):
        return pl.ds(pl.multiple_of(j * tk, tk), tk)

    def scores(j, slot, row0=0):
        rows = key_rows(j)
        kr = kr_ref[rows, :]
        for n, g in enumerate(heads):
            kb = jnp.concatenate([kn_ref[rows, g], kr], axis=-1)
            s_scr[slot, n, row0:, :] = lax.dot_general(q[n][row0:], kb, (((1,), (1,)), ((), ())),
                                                       preferred_element_type=F32)

    def softmax_pv(j, slot, row0=0, key0=None):
        rows = key_rows(j)
        live = [r for r in chunks if r.start >= row0]
        for n, g in enumerate(heads):
            for r in live:
                s = s_scr[slot, n, r, :]
                if key0 is not None:
                    qc = (r.start + lax.broadcasted_iota(jnp.int32, s.shape, 0)) // CHUNK
                    kc = (key0 + lax.broadcasted_iota(jnp.int32, s.shape, 1)) // CHUNK
                    s = jnp.where(kc <= qc, s, -1e30)
                m_old = m_scr[n, r, :]
                m_new = jnp.maximum(m_old, jnp.max(s, axis=-1, keepdims=True))
                alpha = jnp.exp2(m_old - m_new)
                if tk % LANES == 0:
                    p = jnp.exp2(s - jnp.concatenate([m_new] * (tk // LANES), axis=1))
                else:
                    p = jnp.exp2(s - m_new[:, :1])
                p_scr[slot, n, r, :] = p.astype(p_scr.dtype)
                m_scr[n, r, :] = m_new
                l_scr[n, r, :] = alpha * l_scr[n, r, :] + jnp.sum(p, axis=-1, keepdims=True)
                a_scr[slot, n, r, :] = alpha
            pv = jnp.dot(p_scr[slot, n, row0:, :], v_ref[rows, g], preferred_element_type=F32)
            for r in live:
                acc_scr[n, r, :] = a_scr[slot, n, r, :] * acc_scr[n, r, :] + pv[r.start - row0:r.stop - row0]

    slots = s_scr.shape[0]

    def body(jj, carry):
        for u in range(slots):
            scores(jj * slots + u, u)
        for u in range(slots):
            softmax_pv(jj * slots + u, u)
        return carry

    if causal:
        lax.fori_loop(0, pl.program_id(2), body, 0)
        first_diag = pl.program_id(2) * slots
        for d in range(slots):
            scores(first_diag + d, d, row0=d * tk)
        for d in range(slots):
            softmax_pv(first_diag + d, d, row0=d * tk, key0=d * tk)
    else:
        lax.fori_loop(0, kn_ref.shape[0] // (tk * slots), body, 0)
    for n, g in enumerate(heads):
        o_ref[:, g] = (acc_scr[n] / l_scr[n]).astype(o_ref.dtype)


def _attention(qn, qr, kn, kr, v, batch, tq_len, tk_len, heads, causal):
    if causal:
        tk = _tile(tk_len, 512)
        tq = _tile(tq_len, 1024, tk)
        assert tq_len == tk_len and tk % CHUNK == 0 and tq % tk == 0
    else:
        tq, tk = tq_len, tk_len
    nq = tq_len // tq
    slots = tq // tk if causal else 1
    assert heads % HEAD_GROUP == 0
    hg = heads // HEAD_GROUP
    wg = HEAD_GROUP * LANES
    return pl.pallas_call(
        functools.partial(_attention_kernel, tk=tk, causal=causal),
        grid=(batch, hg, nq),
        in_specs=[pl.BlockSpec((tq, wg), lambda b, h, i: (b * nq + i, h)),
                  pl.BlockSpec((tq, wg), lambda b, h, i: (b * nq + i, h)),
                  pl.BlockSpec((tk_len, wg), lambda b, h, i: (b, h)),
                  pl.BlockSpec((tk_len, LANES), lambda b, h, i: (b, 0)),
                  pl.BlockSpec((tk_len, wg), lambda b, h, i: (b, h))],
        out_specs=pl.BlockSpec((tq, wg), lambda b, h, i: (b * nq + i, h)),
        out_shape=jax.ShapeDtypeStruct((batch * tq_len, heads * LANES), BF16),
        scratch_shapes=[pltpu.VMEM((slots, HEAD_GROUP, tq, tk), F32),
                        pltpu.VMEM((slots, HEAD_GROUP, tq, tk), BF16),
                        pltpu.VMEM((HEAD_GROUP, tq, LANES), F32),
                        pltpu.VMEM((HEAD_GROUP, tq, LANES), F32),
                        pltpu.VMEM((slots, HEAD_GROUP, tq, LANES), F32),
                        pltpu.VMEM((HEAD_GROUP, tq, LANES), F32)],
        compiler_params=_params("parallel", "parallel", "arbitrary"),
        name="attention",
    )(qn, qr, kn, kr, v)


def _rope_tables_split(pos, dim):
    inv = ROPE_BASE ** (-jnp.arange(0, dim, 2, dtype=F32) / dim)
    ang = pos.astype(F32)[:, None] * inv[None, :]
    return jnp.cos(ang), jnp.sin(ang)


def _rope_tables_padded(pos, dim):
    cos, sin = _rope_tables_split(pos, dim)
    pad = jnp.zeros((pos.shape[0], LANES - dim), F32)
    return jnp.concatenate([cos, cos, pad], axis=1), jnp.concatenate([-sin, sin, pad], axis=1)


def _pad_lanes(g):
    return jnp.concatenate([g, jnp.zeros((LANES - g.shape[0],), g.dtype)]).reshape(1, LANES)


def _prepare_weights(lw, dims):
    ret_w, q_lora, kv_lora, rope_dim, heads, nope, vdim = dims
    w_in = lw['w_in']
    w_uq = lw['w_uq'].reshape(q_lora, heads, nope + rope_dim)
    w_uq_r = jnp.pad(w_uq[:, :, nope:], ((0, 0), (0, 0), (0, LANES - rope_dim)))
    w_ukv = lw['w_ukv'].reshape(kv_lora, heads, nope + vdim)
    return dict(
        w1=(lw['w1_gate'].astype(BF16), lw['w1_up'].astype(BF16), lw['w1_down'].astype(BF16)),
        w2=(lw['w2_gate'].astype(BF16), lw['w2_up'].astype(BF16), lw['w2_down'].astype(BF16)),
        w_in=w_in.astype(BF16),
        w_uq=jnp.concatenate([w_uq[:, :, :nope].reshape(q_lora, heads * nope),
                              w_uq_r.reshape(q_lora, heads * LANES)], axis=1).astype(BF16),
        w_ukv=jnp.concatenate([w_ukv[:, :, :nope].reshape(kv_lora, heads * nope),
                               w_ukv[:, :, nope:].reshape(kv_lora, heads * vdim)], axis=1).astype(BF16),
        w_out=lw['w_out'].astype(BF16),
        g_qr=_pad_lanes(lw['g_qr']), g_kr=_pad_lanes(lw['g_kr']),
    )


def _layer(x, lw, pw, dims, batch, seq, pos, ret_chunk, state=None, cache_c=None, cache_kr=None):
    ret_w, q_lora, kv_lora, rope_dim, heads, nope, vdim = dims
    ret_heads, dk = lw['ret_heads'], lw['ret_dk']
    q_scale = (nope + rope_dim) ** -0.5 * LOG2_E

    h1 = _ffn(x, lw['g_ffn1'], *pw['w1'])
    xn = _rmsnorm(h1, lw['g_mix'])
    z_main = _matmul([(xn, pw['w_in'], 0)], BF16, n=4 * ret_w)
    n_lat = q_lora + kv_lora + LANES
    z_lat = _matmul([(xn, pw['w_in'], 0)], F32, n=n_lat + (-n_lat) % LAT_TN, col0=4 * ret_w, tn_pref=LAT_TN)

    cos_r, sin_r = _rope_tables_split(pos, dk)
    log_g = jnp.log1p(-jnp.exp2(-5.0 - jnp.arange(ret_heads, dtype=F32)))
    ret_o, s_new = _retention(z_main, lw['g_ret'], log_g, cos_r, sin_r, batch, seq, ret_heads, dk,
                              ret_chunk, state)

    cos_m, sin_m = _rope_tables_padded(pos, rope_dim)
    qa, c_new, c_bf, kr_new, kr_bf = _lat_prep(z_lat, lw['g_qa'], lw['g_kva'], pw['g_kr'], cos_m, sin_m,
                                                seq, q_lora, kv_lora, rope_dim)
    hw = heads * LANES
    qn = _head_matmul(qa, pw['w_uq'], 0, hw, lw['g_qn'], q_scale)
    qr = _head_matmul(qa, pw['w_uq'], hw, hw, pw['g_qr'], q_scale, seq, cos_m, sin_m, rope_dim)
    if cache_c is None:
        keys, c_all, kr_all = seq, c_bf, kr_bf
    else:
        keys = cache_c.shape[1] + seq
        c_all = jnp.concatenate([cache_c.astype(BF16), c_bf.reshape(batch, seq, kv_lora)],
                                axis=1).reshape(batch * keys, kv_lora)
        kr_pad = jnp.pad(cache_kr, ((0, 0), (0, 0), (0, LANES - rope_dim))).astype(BF16)
        kr_all = jnp.concatenate([kr_pad, kr_bf.reshape(batch, seq, LANES)], axis=1).reshape(batch * keys, LANES)
    kn = _head_matmul(c_all, pw['w_ukv'], 0, hw, lw['g_kn'])
    v = _matmul([(c_all, pw['w_ukv'], 0)], BF16, n=hw, col0=hw)
    mla_o = _attention(qn, qr, kn, kr_all, v, batch, seq, keys, heads, causal=cache_c is None)

    h2 = _matmul([(ret_o, pw['w_out'], 0), (mla_o, pw['w_out'], ret_w)], F32, res=h1)
    y = _ffn(h2, lw['g_ffn2'], *pw['w2'], g_final=lw['g_final'])
    return y, s_new, c_new, kr_new


def kernel(x_prompt, x_sample, state_ret, cache_ckv, cache_krope,
           g_ffn1, w1_gate, w1_up, w1_down, g_mix, w_in, g_ret,
           g_qa, w_uq, g_qn, g_qr, g_kva, g_kr, w_ukv, g_kn, w_out,
           g_ffn2, w2_gate, w2_up, w2_down, g_final):
    batch, seq, d_model = x_prompt.shape
    dec_batch, dec_seq, _ = x_sample.shape
    depth, _, ret_heads, dk, dv = state_ret.shape
    past_len, kv_lora = cache_ckv.shape[2], cache_ckv.shape[3]
    rope_dim = cache_krope.shape[3]
    q_lora = g_qa.shape[1]
    nope = g_qn.shape[1]
    heads = w_uq.shape[2] // (nope + rope_dim)
    vdim = w_ukv.shape[2] // heads - nope
    ret_w = ret_heads * dk
    assert dk == dv and nope == LANES and vdim == LANES and rope_dim <= LANES // 2
    assert w_in.shape[2] == 4 * ret_w + q_lora + kv_lora + rope_dim
    dims = (ret_w, q_lora, kv_lora, rope_dim, heads, nope, vdim)

    pos_p = jnp.arange(seq)
    pos_s = past_len + jnp.arange(dec_seq)
    xp = x_prompt.reshape(batch * seq, d_model)
    xs = x_sample.reshape(dec_batch * dec_seq, d_model)
    outs = [[] for _ in range(6)]
    for l in range(depth):
        lw = dict(g_ffn1=g_ffn1[l], w1_gate=w1_gate[l], w1_up=w1_up[l], w1_down=w1_down[l],
                  g_mix=g_mix[l], w_in=w_in[l], g_ret=g_ret[l], g_qa=g_qa[l], w_uq=w_uq[l],
                  g_qn=g_qn[l], g_qr=g_qr[l], g_kva=g_kva[l], g_kr=g_kr[l], w_ukv=w_ukv[l],
                  g_kn=g_kn[l], w_out=w_out[l], g_ffn2=g_ffn2[l], w2_gate=w2_gate[l],
                  w2_up=w2_up[l], w2_down=w2_down[l], g_final=g_final[l],
                  ret_heads=ret_heads, ret_dk=dk)
        pw = _prepare_weights(lw, dims)
        xp, s_p, c_p, k_p = _layer(xp, lw, pw, dims, batch, seq, pos_p, _tile(seq, 256))
        xs, s_s, c_s, k_s = _layer(xs, lw, pw, dims, dec_batch, dec_seq, pos_s, dec_seq,
                                   state_ret[l], cache_ckv[l], cache_krope[l])
        outs[0].append(s_p)
        outs[1].append(c_p.reshape(batch, seq, kv_lora))
        outs[2].append(k_p.reshape(batch, seq, rope_dim))
        outs[3].append(s_s)
        outs[4].append(c_s.reshape(dec_batch, dec_seq, kv_lora))
        outs[5].append(k_s.reshape(dec_batch, dec_seq, rope_dim))
    return (xp.reshape(batch, seq, d_model), xs.reshape(dec_batch, dec_seq, d_model),
            *[jnp.stack(o) for o in outs])
```

```python
import functools

import jax
import jax.numpy as jnp
from jax import lax
from jax.experimental import pallas as pl
from jax.experimental.pallas import tpu as pltpu

F32 = jnp.float32
BF16 = jnp.bfloat16

EPS = 1e-6
ROPE_BASE = 10000.0
CHUNK = 64

LANES = 128
VMEM_LIMIT_BYTES = 56 * 1024 * 1024
LAT_TN = 256
LOG2_E = 1.4426950408889634
HEAD_GROUP = 2
RET_GROUP = 2
HEAD_ROWS = 256
SOFTMAX_ROWS = 32
FFN_ROWS = 512
ROW_CHUNK = 128


def _params(*sem):
    return pltpu.CompilerParams(dimension_semantics=sem, vmem_limit_bytes=VMEM_LIMIT_BYTES)


def _tile(n, pref, mult=8):
    if n <= pref:
        return n
    t = (pref // mult) * mult
    while t >= mult:
        if n % t == 0:
            return t
        t -= mult
    return n


def _rms(x, g):
    return x * lax.rsqrt(jnp.mean(x * x, axis=-1, keepdims=True) + EPS) * g


def _rmsnorm_kernel(x_ref, g_ref, o_ref):
    o_ref[...] = _rms(x_ref[...], g_ref[...]).astype(o_ref.dtype)


def _rmsnorm(x, g):
    m, d = x.shape
    tm = _tile(m, 256)
    return pl.pallas_call(
        _rmsnorm_kernel,
        grid=(m // tm,),
        in_specs=[pl.BlockSpec((tm, d), lambda i: (i, 0)),
                  pl.BlockSpec((1, d), lambda i: (0, 0))],
        out_specs=pl.BlockSpec((tm, d), lambda i: (i, 0)),
        out_shape=jax.ShapeDtypeStruct((m, d), BF16),
        compiler_params=_params("parallel"),
        name="rmsnorm",
    )(x, g.reshape(1, d))


def _ffn_kernel(x_hbm, g_ref, wg_ref, wu_ref, wd_ref, gf_ref, o_ref, *rest, final_norm, tn, cast):
    if cast:
        wg_out, wu_out, wd_out, xn_ref, x_sems = rest
        wg_out[...] = wg_ref[...].astype(BF16)
        wu_out[...] = wu_ref[...].astype(BF16)
        wd_out[...] = wd_ref[...].astype(BF16)
        wg_ref, wu_ref, wd_ref = wg_out, wu_out, wd_out
    else:
        xn_ref, x_sems = rest
    i = pl.program_id(0)
    j = pl.program_id(1)
    tm, d = o_ref.shape
    row_chunks = [(r, min(ROW_CHUNK, tm - r)) for r in range(0, tm, ROW_CHUNK)]

    def x_copy(k):
        r, n = row_chunks[k]
        return pltpu.make_async_copy(x_hbm.at[pl.ds(i * tm + r, n), :], o_ref.at[pl.ds(r, n), :], x_sems.at[k])

    @pl.when(j == 0)
    def _():
        for k in range(len(row_chunks)):
            x_copy(k).start()
        for k, (r, n) in enumerate(row_chunks):
            x_copy(k).wait()
            xn_ref[r:r + n, :] = _rms(o_ref[r:r + n, :], g_ref[...]).astype(BF16)

    for r0 in range(0, tm, FFN_ROWS):
        rows = slice(r0, min(r0 + FFN_ROWS, tm))
        xn = xn_ref[rows, :]
        a = jnp.dot(xn, wg_ref[...], preferred_element_type=F32)
        b = jnp.dot(xn, wu_ref[...], preferred_element_type=F32)
        h = (0.5 * (a * (1.0 / (1.0 + jnp.exp(-a)))) * b).astype(BF16)
        for n in range(0, d, tn):
            o_ref[rows, n:n + tn] += jnp.dot(h, wd_ref[:, n:n + tn], preferred_element_type=F32)

    if final_norm:
        @pl.when(j == pl.num_programs(1) - 1)
        def _():
            for r, n in row_chunks:
                o_ref[r:r + n, :] = _rms(o_ref[r:r + n, :], gf_ref[...])


def _ffn(x, g, wg, wu, wd, g_final=None):
    m, d = x.shape
    f = wg.shape[1]
    cast = wg.dtype == F32
    tm = _tile(m, 512 if cast else 1024)
    tf = _tile(f, 256, LANES)
    final_norm = g_final is not None
    gf = (g_final if final_norm else g).reshape(1, d)
    w_specs = [pl.BlockSpec((d, tf), lambda i, j: (0, j)),
               pl.BlockSpec((d, tf), lambda i, j: (0, j)),
               pl.BlockSpec((tf, d), lambda i, j: (j, 0))]
    out_specs = pl.BlockSpec((tm, d), lambda i, j: (i, 0))
    out_shape = jax.ShapeDtypeStruct((m, d), F32)
    if cast:
        assert m == tm
        out_specs = [pl.BlockSpec((tm, d), lambda i, j: (i, 0), pipeline_mode=pl.Buffered(1))] + w_specs
        out_shape = [out_shape] + [jax.ShapeDtypeStruct(w.shape, BF16) for w in (wg, wu, wd)]
    out = pl.pallas_call(
        functools.partial(_ffn_kernel, final_norm=final_norm, tn=_tile(d, 512, LANES), cast=cast),
        grid=(m // tm, f // tf),
        in_specs=[pl.BlockSpec(memory_space=pl.ANY),
                  pl.BlockSpec((1, d), lambda i, j: (0, 0)),
                  *w_specs,
                  pl.BlockSpec((1, d), lambda i, j: (0, 0))],
        out_specs=out_specs,
        out_shape=out_shape,
        scratch_shapes=[pltpu.VMEM((tm, d), BF16),
                        pltpu.SemaphoreType.DMA((-(-tm // ROW_CHUNK),))],
        compiler_params=_params("parallel", "arbitrary"),
        name="ffn_cast" if cast else "ffn",
    )(x, g.reshape(1, d), wg, wu, wd, gf)
    return (out[0], tuple(out[1:])) if cast else out


def _matmul_kernel(*refs, n_pairs, has_res, n_valid):
    o_ref = refs[-1]
    acc = None
    for p in range(n_pairs):
        part = jnp.dot(refs[2 * p][...], refs[2 * p + 1][...], preferred_element_type=F32)
        acc = part if acc is None else acc + part
    if has_res:
        acc = refs[2 * n_pairs][...] + acc
    if n_valid is not None:
        col = pl.program_id(1) * o_ref.shape[1] + lax.broadcasted_iota(jnp.int32, acc.shape, 1)
        acc = jnp.where(col < n_valid, acc, 0.0)
    o_ref[...] = acc.astype(o_ref.dtype)


def _matmul(pairs, out_dtype, res=None, n=None, col0=0, tm_pref=1024, tn_pref=512):
    m = pairs[0][0].shape[0]
    n = pairs[0][1].shape[1] if n is None else n
    tm = _tile(m, tm_pref)
    tn = _tile(n, tn_pref, LANES)
    assert col0 % tn == 0
    in_specs, args = [], []
    for x, w, row0 in pairs:
        k = x.shape[1]
        assert row0 % k == 0
        in_specs += [pl.BlockSpec((tm, k), lambda i, j: (i, 0)),
                     pl.BlockSpec((k, tn), lambda i, j, rb=row0 // k: (rb, col0 // tn + j))]
        args += [x, w]
    if res is not None:
        in_specs.append(pl.BlockSpec((tm, tn), lambda i, j: (i, j)))
        args.append(res)
    w_cols = pairs[0][1].shape[1] - col0
    return pl.pallas_call(
        functools.partial(_matmul_kernel, n_pairs=len(pairs), has_res=res is not None,
                          n_valid=w_cols if w_cols < n else None),
        grid=(m // tm, n // tn),
        in_specs=in_specs,
        out_specs=pl.BlockSpec((tm, tn), lambda i, j: (i, j)),
        out_shape=jax.ShapeDtypeStruct((m, n), out_dtype),
        compiler_params=_params("parallel", "parallel"),
        name="matmul",
    )(*args)


def _retention_kernel(lg_ref, rq_ref, rk_ref, rv_ref, rg_ref, cos_ref, sin_ref, gain_ref, *rest,
                      has_state, k_scale):
    if has_state:
        s0_ref, o_ref, s_out_ref, s_scr, dmask_scr = rest
    else:
        o_ref, s_out_ref, s_scr, dmask_scr = rest
    c = pl.program_id(2)
    length = o_ref.shape[0]
    group = s_scr.shape[0]
    dk = s_scr.shape[1]
    ri = lax.broadcasted_iota(jnp.int32, (length, 1), 0).astype(F32)
    log_g = [lg_ref[pl.program_id(1) * group + n] for n in range(group)]

    @pl.when(c == 0)
    def _():
        diff = ri - lax.broadcasted_iota(jnp.int32, (1, length), 1).astype(F32)
        for n in range(group):
            if has_state:
                s_scr[n] = s0_ref[0, n]
            else:
                s_scr[n] = jnp.zeros((dk, dk), F32)
            dmask_scr[n] = jnp.where(diff >= 0, jnp.exp(log_g[n] * diff), 0.0)

    pos = pl.ds(pl.multiple_of(c * length, length), length)
    cos = cos_ref[pos, :]
    sin = sin_ref[pos, :]
    half = cos.shape[1]

    def rope(x):
        x = x.astype(F32)
        x1, x2 = x[:, :half], x[:, half:]
        return jnp.concatenate([x1 * cos - x2 * sin, x2 * cos + x1 * sin], axis=-1)

    for n in range(group):
        cols = slice(n * dk, (n + 1) * dk)
        lg = log_g[n]
        q = rope(rq_ref[:, cols])
        k = rope(rk_ref[:, cols]) * k_scale
        v = rv_ref[:, cols]
        inner = lax.dot_general(q.astype(BF16), k.astype(BF16), (((1,), (1,)), ((), ())),
                                preferred_element_type=F32) * dmask_scr[n]
        o = jnp.dot(inner.astype(BF16), v, preferred_element_type=F32)
        s = s_scr[n]
        q_dec = jnp.exp(lg * (ri + 1.0))
        o = o + jnp.dot((q * q_dec).astype(BF16), s.astype(BF16), preferred_element_type=F32)
        k_dec = jnp.exp(lg * (length - 1.0 - ri))
        s_new = jnp.exp(jnp.full((1, 1), lg * length, F32)) * s + lax.dot_general(
            (k * k_dec).astype(BF16), v, (((0,), (0,)), ((), ())), preferred_element_type=F32)
        s_scr[n] = s_new

        mu = jnp.mean(o, axis=-1, keepdims=True)
        oc = o - mu
        var = jnp.mean(oc * oc, axis=-1, keepdims=True)
        y = oc * lax.rsqrt(var + EPS) * gain_ref[:, cols]
        gate = rg_ref[:, cols].astype(F32)
        o_ref[:, cols] = ((gate * (1.0 / (1.0 + jnp.exp(-gate)))) * y).astype(o_ref.dtype)

    @pl.when(c == pl.num_programs(2) - 1)
    def _():
        s_out_ref[0] = s_scr[...]


def _retention(z, gain, log_g, cos, sin, batch, seq, heads, dk, chunk, state=None):
    assert dk == 2 * cos.shape[1] and dk % LANES == 0 and heads % RET_GROUP == 0
    nc = seq // chunk
    has_state = state is not None
    hg = heads // RET_GROUP
    wg = RET_GROUP * dk

    def strip(part):
        return pl.BlockSpec((chunk, wg), lambda b, h, c, lg: (b * nc + c, part * hg + h))

    in_specs = [strip(0), strip(1), strip(2), strip(3),
                pl.BlockSpec((seq, dk // 2), lambda b, h, c, lg: (0, 0)),
                pl.BlockSpec((seq, dk // 2), lambda b, h, c, lg: (0, 0)),
                pl.BlockSpec((1, wg), lambda b, h, c, lg: (0, h))]
    args = [z, z, z, z, cos, sin, gain.reshape(1, heads * dk)]
    if has_state:
        in_specs.append(pl.BlockSpec((1, RET_GROUP, dk, dk), lambda b, h, c, lg: (b, h, 0, 0)))
        args.append(state)
    return pl.pallas_call(
        functools.partial(_retention_kernel, has_state=has_state, k_scale=dk ** -0.5),
        grid_spec=pltpu.PrefetchScalarGridSpec(
            num_scalar_prefetch=1,
            grid=(batch, hg, nc),
            in_specs=in_specs,
            out_specs=[pl.BlockSpec((chunk, wg), lambda b, h, c, lg: (b * nc + c, h)),
                       pl.BlockSpec((1, RET_GROUP, dk, dk), lambda b, h, c, lg: (b, h, 0, 0))],
            scratch_shapes=[pltpu.VMEM((RET_GROUP, dk, dk), F32), pltpu.VMEM((RET_GROUP, chunk, chunk), F32)]),
        out_shape=[jax.ShapeDtypeStruct((batch * seq, heads * dk), BF16),
                   jax.ShapeDtypeStruct((batch, heads, dk, dk), F32)],
        compiler_params=_params("parallel", "parallel", "arbitrary"),
        name="retention",
    )(log_g, *args)


def _rope_padded(y, cos, sin_signed, rot):
    lane = lax.broadcasted_iota(jnp.int32, y.shape, 1)
    swapped = jnp.where(lane < rot, pltpu.roll(y, LANES - rot, 1), pltpu.roll(y, rot, 1))
    return y * cos + swapped * sin_signed


def _lat_prep_kernel(z_ref, gqa_ref, gkva_ref, gkr_ref, cos_ref, sin_ref,
                     qa_ref, c32_ref, c16_ref, kr32_ref, kr16_ref, *, q_lora, kv_lora, rope_dim):
    z = z_ref[...]
    qa_ref[...] = _rms(z[:, :q_lora], gqa_ref[...]).astype(qa_ref.dtype)
    c = _rms(z[:, q_lora:q_lora + kv_lora], gkva_ref[...])
    c32_ref[...] = c
    c16_ref[...] = c.astype(c16_ref.dtype)
    r = z[:, q_lora + kv_lora:q_lora + kv_lora + LANES]
    y = r * lax.rsqrt(jnp.sum(r * r, axis=-1, keepdims=True) * (1.0 / rope_dim) + EPS) * gkr_ref[...]
    kr = _rope_padded(y, cos_ref[...], sin_ref[...], rope_dim // 2)
    kr32_ref[...] = kr[:, :rope_dim]
    kr16_ref[...] = kr.astype(kr16_ref.dtype)


def _lat_prep(z_lat, g_qa, g_kva, g_kr_pad, cos, sin, seq, q_lora, kv_lora, rope_dim):
    m, n = z_lat.shape
    tm = _tile(seq, 256)
    nb = seq // tm
    row = lambda i: (i, 0)
    const = lambda i: (0, 0)
    pos = lambda i: (i % nb, 0)
    return pl.pallas_call(
        functools.partial(_lat_prep_kernel, q_lora=q_lora, kv_lora=kv_lora, rope_dim=rope_dim),
        grid=(m // tm,),
        in_specs=[pl.BlockSpec((tm, n), row),
                  pl.BlockSpec((1, q_lora), const),
                  pl.BlockSpec((1, kv_lora), const),
                  pl.BlockSpec((1, LANES), const),
                  pl.BlockSpec((tm, LANES), pos),
                  pl.BlockSpec((tm, LANES), pos)],
        out_specs=[pl.BlockSpec((tm, q_lora), row),
                   pl.BlockSpec((tm, kv_lora), row),
                   pl.BlockSpec((tm, kv_lora), row),
                   pl.BlockSpec((tm, rope_dim), row),
                   pl.BlockSpec((tm, LANES), row)],
        out_shape=[jax.ShapeDtypeStruct((m, q_lora), BF16),
                   jax.ShapeDtypeStruct((m, kv_lora), F32),
                   jax.ShapeDtypeStruct((m, kv_lora), BF16),
                   jax.ShapeDtypeStruct((m, rope_dim), F32),
                   jax.ShapeDtypeStruct((m, LANES), BF16)],
        compiler_params=_params("parallel"),
        name="lat_prep",
    )(z_lat, g_qa.reshape(1, -1), g_kva.reshape(1, -1), g_kr_pad, cos, sin)


def _head_matmul_kernel(x_ref, w_ref, g_ref, *rest, rope_dim, scale):
    o_ref = rest[-1]
    groups = [slice(lo, lo + LANES) for lo in range(0, o_ref.shape[1], LANES)]
    for r0 in range(0, o_ref.shape[0], HEAD_ROWS):
        rows = slice(r0, min(r0 + HEAD_ROWS, o_ref.shape[0]))
        acc = jnp.dot(x_ref[rows, :], w_ref[...], preferred_element_type=F32)
        for g in groups:
            a = acc[:, g]
            if rope_dim is None:
                y = _rms(a, g_ref[...])
            else:
                cos_ref, sin_ref = rest[:2]
                y = a * lax.rsqrt(jnp.sum(a * a, axis=-1, keepdims=True) * (1.0 / rope_dim) + EPS) * g_ref[...]
                y = _rope_padded(y, cos_ref[rows, :], sin_ref[rows, :], rope_dim // 2)
            o_ref[rows, g] = (y * scale).astype(o_ref.dtype)


def _head_matmul(x, w, col0, n, g, scale=1.0, seq=None, cos=None, sin=None, rope_dim=None):
    m, k = x.shape
    tm = _tile(m, 1024)
    tn = _tile(n, 512, LANES)
    assert col0 % tn == 0
    const = lambda i, j: (0, 0)
    in_specs = [pl.BlockSpec((tm, k), lambda i, j: (i, 0)),
                pl.BlockSpec((k, tn), lambda i, j: (0, col0 // tn + j)),
                pl.BlockSpec((1, LANES), const)]
    args = [x, w, g.reshape(1, LANES)]
    if rope_dim is not None:
        if tm >= seq:
            assert tm % seq == 0
            cos, sin = jnp.tile(cos, (tm // seq, 1)), jnp.tile(sin, (tm // seq, 1))
            nb = 1
        else:
            assert seq % tm == 0
            nb = seq // tm
        pos = lambda i, j: (i % nb, 0)
        in_specs += [pl.BlockSpec((tm, LANES), pos), pl.BlockSpec((tm, LANES), pos)]
        args += [cos, sin]
    return pl.pallas_call(
        functools.partial(_head_matmul_kernel, rope_dim=rope_dim, scale=scale),
        grid=(m // tm, n // tn),
        in_specs=in_specs,
        out_specs=pl.BlockSpec((tm, tn), lambda i, j: (i, j)),
        out_shape=jax.ShapeDtypeStruct((m, n), BF16),
        compiler_params=_params("parallel", "parallel"),
        name="head_matmul",
    )(*args)


def _attention_kernel(qn_ref, qr_ref, kn_ref, kr_ref, v_ref, o_ref,
                      s_scr, p_scr, m_scr, l_scr, a_scr, acc_scr, *, tk, causal):
    tq = qn_ref.shape[0]
    heads = [slice(g * LANES, (g + 1) * LANES) for g in range(qn_ref.shape[1] // LANES)]
    q = [jnp.concatenate([qn_ref[:, g], qr_ref[:, g]], axis=-1) for g in heads]
    chunks = [slice(r, min(r + SOFTMAX_ROWS, tq)) for r in range(0, tq, SOFTMAX_ROWS)]
    m_scr[...] = jnp.full(m_scr.shape, -1e30, F32)
    l_scr[...] = jnp.zeros(l_scr.shape, F32)
    acc_scr[...] = jnp.zeros(acc_scr.shape, F32)

    def key_rows(j):
        return pl.ds(pl.multiple_of(j * tk, tk), tk)

    def scores(j, slot, row0=0):
        rows = key_rows(j)
        kr = kr_ref[rows, :]
        for n, g in enumerate(heads):
            kb = jnp.concatenate([kn_ref[rows, g], kr], axis=-1)
            s_scr[slot, n, row0:, :] = lax.dot_general(q[n][row0:], kb, (((1,), (1,)), ((), ())),
                                                       preferred_element_type=F32)

    def softmax_pv(j, slot, row0=0, key0=None):
        rows = key_rows(j)
        live = [r for r in chunks if r.start >= row0]
        for n, g in enumerate(heads):
            for r in live:
                s = s_scr[slot, n, r, :]
                if key0 is not None:
                    qc = (r.start + lax.broadcasted_iota(jnp.int32, s.shape, 0)) // CHUNK
                    kc = (key0 + lax.broadcasted_iota(jnp.int32, s.shape, 1)) // CHUNK
                    s = jnp.where(kc <= qc, s, -1e30)
                m_old = m_scr[n, r, :]
                m_new = jnp.maximum(m_old, jnp.max(s, axis=-1, keepdims=True))
                alpha = jnp.exp2(m_old - m_new)
                if tk % LANES == 0:
                    p = jnp.exp2(s - jnp.concatenate([m_new] * (tk // LANES), axis=1))
                else:
                    p = jnp.exp2(s - m_new[:, :1])
                p_scr[slot, n, r, :] = p.astype(p_scr.dtype)
                m_scr[n, r, :] = m_new
                l_scr[n, r, :] = alpha * l_scr[n, r, :] + jnp.sum(p, axis=-1, keepdims=True)
                a_scr[slot, n, r, :] = alpha
            pv = jnp.dot(p_scr[slot, n, row0:, :], v_ref[rows, g], preferred_element_type=F32)
            for r in live:
                acc_scr[n, r, :] = a_scr[slot, n, r, :] * acc_scr[n, r, :] + pv[r.start - row0:r.stop - row0]

    slots = s_scr.shape[0]

    def body(jj, carry):
        for u in range(slots):
            scores(jj * slots + u, u)
        for u in range(slots):
            softmax_pv(jj * slots + u, u)
        return carry

    if causal:
        lax.fori_loop(0, pl.program_id(2), body, 0)
        first_diag = pl.program_id(2) * slots
        for d in range(slots):
            scores(first_diag + d, d, row0=d * tk)
        for d in range(slots):
            softmax_pv(first_diag + d, d, row0=d * tk, key0=d * tk)
    else:
        lax.fori_loop(0, kn_ref.shape[0] // (tk * slots), body, 0)
    for n, g in enumerate(heads):
        o_ref[:, g] = (acc_scr[n] / l_scr[n]).astype(o_ref.dtype)


def _attention(qn, qr, kn, kr, v, batch, tq_len, tk_len, heads, causal):
    if causal:
        tk = _tile(tk_len, 512)
        tq = _tile(tq_len, 1024, tk)
        assert tq_len == tk_len and tk % CHUNK == 0 and tq % tk == 0
    else:
        tq, tk = tq_len, tk_len
    nq = tq_len // tq
    slots = tq // tk if causal else 1
    assert heads % HEAD_GROUP == 0
    hg = heads // HEAD_GROUP
    wg = HEAD_GROUP * LANES
    return pl.pallas_call(
        functools.partial(_attention_kernel, tk=tk, causal=causal),
        grid=(batch, hg, nq),
        in_specs=[pl.BlockSpec((tq, wg), lambda b, h, i: (b * nq + i, h)),
                  pl.BlockSpec((tq, wg), lambda b, h, i: (b * nq + i, h)),
                  pl.BlockSpec((tk_len, wg), lambda b, h, i: (b, h)),
                  pl.BlockSpec((tk_len, LANES), lambda b, h, i: (b, 0)),
                  pl.BlockSpec((tk_len, wg), lambda b, h, i: (b, h))],
        out_specs=pl.BlockSpec((tq, wg), lambda b, h, i: (b * nq + i, h)),
        out_shape=jax.ShapeDtypeStruct((batch * tq_len, heads * LANES), BF16),
        scratch_shapes=[pltpu.VMEM((slots, HEAD_GROUP, tq, tk), F32),
                        pltpu.VMEM((slots, HEAD_GROUP, tq, tk), BF16),
                        pltpu.VMEM((HEAD_GROUP, tq, LANES), F32),
                        pltpu.VMEM((HEAD_GROUP, tq, LANES), F32),
                        pltpu.VMEM((slots, HEAD_GROUP, tq, LANES), F32),
                        pltpu.VMEM((HEAD_GROUP, tq, LANES), F32)],
        compiler_params=_params("parallel", "parallel", "arbitrary"),
        name="attention",
    )(qn, qr, kn, kr, v)


def _rope_tables_split(pos, dim):
    inv = ROPE_BASE ** (-jnp.arange(0, dim, 2, dtype=F32) / dim)
    ang = pos.astype(F32)[:, None] * inv[None, :]
    return jnp.cos(ang), jnp.sin(ang)


def _rope_tables_padded(pos, dim):
    cos, sin = _rope_tables_split(pos, dim)
    pad = jnp.zeros((pos.shape[0], LANES - dim), F32)
    return jnp.concatenate([cos, cos, pad], axis=1), jnp.concatenate([-sin, sin, pad], axis=1)


def _pad_lanes(g):
    return jnp.concatenate([g, jnp.zeros((LANES - g.shape[0],), g.dtype)]).reshape(1, LANES)


def _prepare_weights(lw, dims):
    ret_w, q_lora, kv_lora, rope_dim, heads, nope, vdim = dims
    w_in = lw['w_in']
    w_uq = lw['w_uq'].reshape(q_lora, heads, nope + rope_dim)
    w_uq_r = jnp.pad(w_uq[:, :, nope:], ((0, 0), (0, 0), (0, LANES - rope_dim)))
    w_ukv = lw['w_ukv'].reshape(kv_lora, heads, nope + vdim)
    return dict(
        w_in=w_in.astype(BF16),
        w_uq=jnp.concatenate([w_uq[:, :, :nope].reshape(q_lora, heads * nope),
                              w_uq_r.reshape(q_lora, heads * LANES)], axis=1).astype(BF16),
        w_ukv=jnp.concatenate([w_ukv[:, :, :nope].reshape(kv_lora, heads * nope),
                               w_ukv[:, :, nope:].reshape(kv_lora, heads * vdim)], axis=1).astype(BF16),
        w_out=lw['w_out'].astype(BF16),
        g_qr=_pad_lanes(lw['g_qr']), g_kr=_pad_lanes(lw['g_kr']),
    )


def _layer(x, lw, pw, ffn_w, dims, batch, seq, pos, ret_chunk, state=None, cache_c=None, cache_kr=None):
    ret_w, q_lora, kv_lora, rope_dim, heads, nope, vdim = dims
    ret_heads, dk = lw['ret_heads'], lw['ret_dk']
    q_scale = (nope + rope_dim) ** -0.5 * LOG2_E
    cast = ffn_w['w1'][0].dtype == F32

    h1 = _ffn(x, lw['g_ffn1'], *ffn_w['w1'])
    h1, w1_bf = h1 if cast else (h1, ffn_w['w1'])
    xn = _rmsnorm(h1, lw['g_mix'])
    z_main = _matmul([(xn, pw['w_in'], 0)], BF16, n=4 * ret_w)
    n_lat = q_lora + kv_lora + LANES
    z_lat = _matmul([(xn, pw['w_in'], 0)], F32, n=n_lat + (-n_lat) % LAT_TN, col0=4 * ret_w, tn_pref=LAT_TN)

    cos_r, sin_r = _rope_tables_split(pos, dk)
    log_g = jnp.log1p(-jnp.exp2(-5.0 - jnp.arange(ret_heads, dtype=F32)))
    ret_o, s_new = _retention(z_main, lw['g_ret'], log_g, cos_r, sin_r, batch, seq, ret_heads, dk,
                              ret_chunk, state)

    cos_m, sin_m = _rope_tables_padded(pos, rope_dim)
    qa, c_new, c_bf, kr_new, kr_bf = _lat_prep(z_lat, lw['g_qa'], lw['g_kva'], pw['g_kr'], cos_m, sin_m,
                                                seq, q_lora, kv_lora, rope_dim)
    hw = heads * LANES
    qn = _head_matmul(qa, pw['w_uq'], 0, hw, lw['g_qn'], q_scale)
    qr = _head_matmul(qa, pw['w_uq'], hw, hw, pw['g_qr'], q_scale, seq, cos_m, sin_m, rope_dim)
    if cache_c is None:
        keys, c_all, kr_all = seq, c_bf, kr_bf
    else:
        keys = cache_c.shape[1] + seq
        c_all = jnp.concatenate([cache_c.astype(BF16), c_bf.reshape(batch, seq, kv_lora)],
                                axis=1).reshape(batch * keys, kv_lora)
        kr_pad = jnp.pad(cache_kr, ((0, 0), (0, 0), (0, LANES - rope_dim))).astype(BF16)
        kr_all = jnp.concatenate([kr_pad, kr_bf.reshape(batch, seq, LANES)], axis=1).reshape(batch * keys, LANES)
    kn = _head_matmul(c_all, pw['w_ukv'], 0, hw, lw['g_kn'])
    v = _matmul([(c_all, pw['w_ukv'], 0)], BF16, n=hw, col0=hw)
    mla_o = _attention(qn, qr, kn, kr_all, v, batch, seq, keys, heads, causal=cache_c is None)

    h2 = _matmul([(ret_o, pw['w_out'], 0), (mla_o, pw['w_out'], ret_w)], F32, res=h1)
    y = _ffn(h2, lw['g_ffn2'], *ffn_w['w2'], g_final=lw['g_final'])
    y, w2_bf = y if cast else (y, ffn_w['w2'])
    return y, s_new, c_new, kr_new, dict(w1=w1_bf, w2=w2_bf)


def kernel(x_prompt, x_sample, state_ret, cache_ckv, cache_krope,
           g_ffn1, w1_gate, w1_up, w1_down, g_mix, w_in, g_ret,
           g_qa, w_uq, g_qn, g_qr, g_kva, g_kr, w_ukv, g_kn, w_out,
           g_ffn2, w2_gate, w2_up, w2_down, g_final):
    batch, seq, d_model = x_prompt.shape
    dec_batch, dec_seq, _ = x_sample.shape
    depth, _, ret_heads, dk, dv = state_ret.shape
    past_len, kv_lora = cache_ckv.shape[2], cache_ckv.shape[3]
    rope_dim = cache_krope.shape[3]
    q_lora = g_qa.shape[1]
    nope = g_qn.shape[1]
    heads = w_uq.shape[2] // (nope + rope_dim)
    vdim = w_ukv.shape[2] // heads - nope
    ret_w = ret_heads * dk
    assert dk == dv and nope == LANES and vdim == LANES and rope_dim <= LANES // 2
    assert w_in.shape[2] == 4 * ret_w + q_lora + kv_lora + rope_dim
    dims = (ret_w, q_lora, kv_lora, rope_dim, heads, nope, vdim)

    pos_p = jnp.arange(seq)
    pos_s = past_len + jnp.arange(dec_seq)
    xp = x_prompt.reshape(batch * seq, d_model)
    xs = x_sample.reshape(dec_batch * dec_seq, d_model)
    outs = [[] for _ in range(6)]
    for l in range(depth):
        lw = dict(g_ffn1=g_ffn1[l], w1_gate=w1_gate[l], w1_up=w1_up[l], w1_down=w1_down[l],
                  g_mix=g_mix[l], w_in=w_in[l], g_ret=g_ret[l], g_qa=g_qa[l], w_uq=w_uq[l],
                  g_qn=g_qn[l], g_qr=g_qr[l], g_kva=g_kva[l], g_kr=g_kr[l], w_ukv=w_ukv[l],
                  g_kn=g_kn[l], w_out=w_out[l], g_ffn2=g_ffn2[l], w2_gate=w2_gate[l],
                  w2_up=w2_up[l], w2_down=w2_down[l], g_final=g_final[l],
                  ret_heads=ret_heads, ret_dk=dk)
        pw = _prepare_weights(lw, dims)
        ffn_w = dict(w1=(w1_gate[l], w1_up[l], w1_down[l]), w2=(w2_gate[l], w2_up[l], w2_down[l]))
        if xs.shape[0] > FFN_ROWS:
            ffn_w = jax.tree.map(lambda w: w.astype(BF16), ffn_w)
        xs, s_s, c_s, k_s, ffn_w = _layer(xs, lw, pw, ffn_w, dims, dec_batch, dec_seq, pos_s, dec_seq,
                                          state_ret[l], cache_ckv[l], cache_krope[l])
        xp, s_p, c_p, k_p, _ = _layer(xp, lw, pw, ffn_w, dims, batch, seq, pos_p, _tile(seq, 256))
        outs[0].append(s_p)
        outs[1].append(c_p.reshape(batch, seq, kv_lora))
        outs[2].append(k_p.reshape(batch, seq, rope_dim))
        outs[3].append(s_s)
        outs[4].append(c_s.reshape(dec_batch, dec_seq, kv_lora))
        outs[5].append(k_s.reshape(dec_batch, dec_seq, rope_dim))
    return (xp.reshape(batch, seq, d_model), xs.reshape(dec_batch, dec_seq, d_model),
            *[jnp.stack(o) for o in outs])
```

```python
import functools

import jax
import jax.numpy as jnp
from jax import lax
from jax.experimental import pallas as pl
from jax.experimental.pallas import tpu as pltpu

F32 = jnp.float32
BF16 = jnp.bfloat16

EPS = 1e-6
ROPE_BASE = 10000.0
CHUNK = 64

LANES = 128
VMEM_LIMIT_BYTES = 56 * 1024 * 1024
LAT_TN = 256
LOG2_E = 1.4426950408889634
HEAD_GROUP = 2
RET_GROUP = 2
HEAD_ROWS = 256
SOFTMAX_ROWS = 32
FFN_ROWS = 512
SHALLOW_ROWS = 4096
FFN_CAST_ROWS = 512
ROW_CHUNK = 128


def _params(*sem):
    return pltpu.CompilerParams(dimension_semantics=sem, vmem_limit_bytes=VMEM_LIMIT_BYTES)


def _tile(n, pref, mult=8):
    if n <= pref:
        return n
    t = (pref // mult) * mult
    while t >= mult:
        if n % t == 0:
            return t
        t -= mult
    return n


def _rms(x, g):
    return x * lax.rsqrt(jnp.mean(x * x, axis=-1, keepdims=True) + EPS) * g


def _rmsnorm_kernel(x_ref, g_ref, o_ref):
    o_ref[...] = _rms(x_ref[...], g_ref[...]).astype(o_ref.dtype)


def _rmsnorm(x, g):
    m, d = x.shape
    tm = _tile(m, 256)
    return pl.pallas_call(
        _rmsnorm_kernel,
        grid=(m // tm,),
        in_specs=[pl.BlockSpec((tm, d), lambda i: (i, 0)),
                  pl.BlockSpec((1, d), lambda i: (0, 0))],
        out_specs=pl.BlockSpec((tm, d), lambda i: (i, 0)),
        out_shape=jax.ShapeDtypeStruct((m, d), BF16),
        compiler_params=_params("parallel"),
        name="rmsnorm",
    )(x, g.reshape(1, d))


def _ffn_kernel(x_hbm, g_ref, wg_ref, wu_ref, wd_ref, gf_ref, o_ref, *rest, final_norm, tn, cast):
    if cast:
        wg_out, wu_out, wd_out, xn_ref, x_sems = rest
        wg_out[...] = wg_ref[...].astype(BF16)
        wu_out[...] = wu_ref[...].astype(BF16)
        wd_out[...] = wd_ref[...].astype(BF16)
        wg_ref, wu_ref, wd_ref = wg_out, wu_out, wd_out
    else:
        xn_ref, x_sems = rest
    i = pl.program_id(0)
    j = pl.program_id(1)
    tm, d = o_ref.shape
    row_chunks = [(r, min(ROW_CHUNK, tm - r)) for r in range(0, tm, ROW_CHUNK)]

    def x_copy(k):
        r, n = row_chunks[k]
        return pltpu.make_async_copy(x_hbm.at[pl.ds(i * tm + r, n), :], o_ref.at[pl.ds(r, n), :], x_sems.at[k])

    @pl.when(j == 0)
    def _():
        for k in range(len(row_chunks)):
            x_copy(k).start()
        for k, (r, n) in enumerate(row_chunks):
            x_copy(k).wait()
            xn_ref[r:r + n, :] = _rms(o_ref[r:r + n, :], g_ref[...]).astype(BF16)

    for r0 in range(0, tm, FFN_ROWS):
        rows = slice(r0, min(r0 + FFN_ROWS, tm))
        xn = xn_ref[rows, :]
        a = jnp.dot(xn, wg_ref[...], preferred_element_type=F32)
        b = jnp.dot(xn, wu_ref[...], preferred_element_type=F32)
        h = (0.5 * (a * (1.0 / (1.0 + jnp.exp(-a)))) * b).astype(BF16)
        for n in range(0, d, tn):
            o_ref[rows, n:n + tn] += jnp.dot(h, wd_ref[:, n:n + tn], preferred_element_type=F32)

    if final_norm:
        @pl.when(j == pl.num_programs(1) - 1)
        def _():
            for r, n in row_chunks:
                o_ref[r:r + n, :] = _rms(o_ref[r:r + n, :], gf_ref[...])


def _ffn(x, g, wg, wu, wd, g_final=None):
    m, d = x.shape
    f = wg.shape[1]
    cast = wg.dtype == F32
    tm = _tile(m, FFN_CAST_ROWS if cast else 1024)
    tf = _tile(f, 256, LANES)
    final_norm = g_final is not None
    gf = (g_final if final_norm else g).reshape(1, d)
    w_specs = [pl.BlockSpec((d, tf), lambda i, j: (0, j)),
               pl.BlockSpec((d, tf), lambda i, j: (0, j)),
               pl.BlockSpec((tf, d), lambda i, j: (j, 0))]
    out_specs = pl.BlockSpec((tm, d), lambda i, j: (i, 0))
    out_shape = jax.ShapeDtypeStruct((m, d), F32)
    if cast:
        assert m == tm
        out_specs = [pl.BlockSpec((tm, d), lambda i, j: (i, 0), pipeline_mode=pl.Buffered(1))] + w_specs
        out_shape = [out_shape] + [jax.ShapeDtypeStruct(w.shape, BF16) for w in (wg, wu, wd)]
    out = pl.pallas_call(
        functools.partial(_ffn_kernel, final_norm=final_norm, tn=_tile(d, 512, LANES), cast=cast),
        grid=(m // tm, f // tf),
        in_specs=[pl.BlockSpec(memory_space=pl.ANY),
                  pl.BlockSpec((1, d), lambda i, j: (0, 0)),
                  *w_specs,
                  pl.BlockSpec((1, d), lambda i, j: (0, 0))],
        out_specs=out_specs,
        out_shape=out_shape,
        scratch_shapes=[pltpu.VMEM((tm, d), BF16),
                        pltpu.SemaphoreType.DMA((-(-tm // ROW_CHUNK),))],
        compiler_params=_params("parallel", "arbitrary"),
        name="ffn_cast" if cast else "ffn",
    )(x, g.reshape(1, d), wg, wu, wd, gf)
    return (out[0], tuple(out[1:])) if cast else out


def _matmul_kernel(*refs, n_pairs, has_res, n_valid, cast):
    if cast:
        o_ref, w_out = refs[-2:]
        w_out[...] = refs[1][...].astype(BF16)
        refs = (refs[0], w_out)
    else:
        o_ref = refs[-1]
    acc = None
    for p in range(n_pairs):
        part = jnp.dot(refs[2 * p][...], refs[2 * p + 1][...], preferred_element_type=F32)
        acc = part if acc is None else acc + part
    if has_res:
        acc = refs[2 * n_pairs][...] + acc
    if n_valid is not None:
        col = pl.program_id(1) * o_ref.shape[1] + lax.broadcasted_iota(jnp.int32, acc.shape, 1)
        acc = jnp.where(col < n_valid, acc, 0.0)
    o_ref[...] = acc.astype(o_ref.dtype)


def _matmul(pairs, out_dtype, res=None, n=None, col0=0, tm_pref=1024, tn_pref=512):
    m = pairs[0][0].shape[0]
    n = pairs[0][1].shape[1] if n is None else n
    tm = _tile(m, tm_pref)
    tn = _tile(n, tn_pref, LANES)
    assert col0 % tn == 0
    cast = pairs[0][1].dtype == F32
    w_cols = pairs[0][1].shape[1] - col0
    out_specs = pl.BlockSpec((tm, tn), lambda i, j: (i, j))
    out_shape = jax.ShapeDtypeStruct((m, n), out_dtype)
    if cast:
        k = pairs[0][0].shape[1]
        assert len(pairs) == 1 and pairs[0][2] == 0 and res is None and m == tm and w_cols >= n
        out_specs = [out_specs, pl.BlockSpec((k, tn), lambda i, j: (0, j))]
        out_shape = [out_shape, jax.ShapeDtypeStruct((k, n), BF16)]
    in_specs, args = [], []
    for x, w, row0 in pairs:
        k = x.shape[1]
        assert row0 % k == 0
        in_specs += [pl.BlockSpec((tm, k), lambda i, j: (i, 0)),
                     pl.BlockSpec((k, tn), lambda i, j, rb=row0 // k: (rb, col0 // tn + j))]
        args += [x, w]
    if res is not None:
        in_specs.append(pl.BlockSpec((tm, tn), lambda i, j: (i, j)))
        args.append(res)
    out = pl.pallas_call(
        functools.partial(_matmul_kernel, n_pairs=len(pairs), has_res=res is not None,
                          n_valid=w_cols if w_cols < n else None, cast=cast),
        grid=(m // tm, n // tn),
        in_specs=in_specs,
        out_specs=out_specs,
        out_shape=out_shape,
        compiler_params=_params("parallel", "parallel"),
        name="matmul_cast" if cast else "matmul",
    )(*args)
    return tuple(out) if cast else out


def _retention_kernel(lg_ref, rq_ref, rk_ref, rv_ref, rg_ref, cos_ref, sin_ref, gain_ref, *rest,
                      has_state, k_scale):
    if has_state:
        s0_ref, o_ref, s_out_ref, s_scr, dmask_scr = rest
    else:
        o_ref, s_out_ref, s_scr, dmask_scr = rest
    c = pl.program_id(2)
    length = o_ref.shape[0]
    group = s_scr.shape[0]
    dk = s_scr.shape[1]
    ri = lax.broadcasted_iota(jnp.int32, (length, 1), 0).astype(F32)
    log_g = [lg_ref[pl.program_id(1) * group + n] for n in range(group)]

    @pl.when(c == 0)
    def _():
        diff = ri - lax.broadcasted_iota(jnp.int32, (1, length), 1).astype(F32)
        for n in range(group):
            if has_state:
                s_scr[n] = s0_ref[0, n]
            else:
                s_scr[n] = jnp.zeros((dk, dk), F32)
            dmask_scr[n] = jnp.where(diff >= 0, jnp.exp(log_g[n] * diff), 0.0)

    pos = pl.ds(pl.multiple_of(c * length, length), length)
    cos = cos_ref[pos, :]
    sin = sin_ref[pos, :]
    half = cos.shape[1]

    def rope(x):
        x = x.astype(F32)
        x1, x2 = x[:, :half], x[:, half:]
        return jnp.concatenate([x1 * cos - x2 * sin, x2 * cos + x1 * sin], axis=-1)

    for n in range(group):
        cols = slice(n * dk, (n + 1) * dk)
        lg = log_g[n]
        q = rope(rq_ref[:, cols])
        k = rope(rk_ref[:, cols]) * k_scale
        v = rv_ref[:, cols]
        inner = lax.dot_general(q.astype(BF16), k.astype(BF16), (((1,), (1,)), ((), ())),
                                preferred_element_type=F32) * dmask_scr[n]
        o = jnp.dot(inner.astype(BF16), v, preferred_element_type=F32)
        s = s_scr[n]
        q_dec = jnp.exp(lg * (ri + 1.0))
        o = o + jnp.dot((q * q_dec).astype(BF16), s.astype(BF16), preferred_element_type=F32)
        k_dec = jnp.exp(lg * (length - 1.0 - ri))
        s_new = jnp.exp(jnp.full((1, 1), lg * length, F32)) * s + lax.dot_general(
            (k * k_dec).astype(BF16), v, (((0,), (0,)), ((), ())), preferred_element_type=F32)
        s_scr[n] = s_new

        mu = jnp.mean(o, axis=-1, keepdims=True)
        oc = o - mu
        var = jnp.mean(oc * oc, axis=-1, keepdims=True)
        y = oc * lax.rsqrt(var + EPS) * gain_ref[:, cols]
        gate = rg_ref[:, cols].astype(F32)
        o_ref[:, cols] = ((gate * (1.0 / (1.0 + jnp.exp(-gate)))) * y).astype(o_ref.dtype)

    @pl.when(c == pl.num_programs(2) - 1)
    def _():
        s_out_ref[0] = s_scr[...]


def _retention(z, gain, log_g, cos, sin, batch, seq, heads, dk, chunk, state=None):
    assert dk == 2 * cos.shape[1] and dk % LANES == 0 and heads % RET_GROUP == 0
    nc = seq // chunk
    has_state = state is not None
    hg = heads // RET_GROUP
    wg = RET_GROUP * dk

    def strip(part):
        return pl.BlockSpec((chunk, wg), lambda b, h, c, lg: (b * nc + c, part * hg + h))

    in_specs = [strip(0), strip(1), strip(2), strip(3),
                pl.BlockSpec((seq, dk // 2), lambda b, h, c, lg: (0, 0)),
                pl.BlockSpec((seq, dk // 2), lambda b, h, c, lg: (0, 0)),
                pl.BlockSpec((1, wg), lambda b, h, c, lg: (0, h))]
    args = [z, z, z, z, cos, sin, gain.reshape(1, heads * dk)]
    if has_state:
        in_specs.append(pl.BlockSpec((1, RET_GROUP, dk, dk), lambda b, h, c, lg: (b, h, 0, 0)))
        args.append(state)
    return pl.pallas_call(
        functools.partial(_retention_kernel, has_state=has_state, k_scale=dk ** -0.5),
        grid_spec=pltpu.PrefetchScalarGridSpec(
            num_scalar_prefetch=1,
            grid=(batch, hg, nc),
            in_specs=in_specs,
            out_specs=[pl.BlockSpec((chunk, wg), lambda b, h, c, lg: (b * nc + c, h)),
                       pl.BlockSpec((1, RET_GROUP, dk, dk), lambda b, h, c, lg: (b, h, 0, 0))],
            scratch_shapes=[pltpu.VMEM((RET_GROUP, dk, dk), F32), pltpu.VMEM((RET_GROUP, chunk, chunk), F32)]),
        out_shape=[jax.ShapeDtypeStruct((batch * seq, heads * dk), BF16),
                   jax.ShapeDtypeStruct((batch, heads, dk, dk), F32)],
        compiler_params=_params("parallel", "parallel", "arbitrary"),
        name="retention",
    )(log_g, *args)


def _rope_padded(y, cos, sin_signed, rot):
    lane = lax.broadcasted_iota(jnp.int32, y.shape, 1)
    swapped = jnp.where(lane < rot, pltpu.roll(y, LANES - rot, 1), pltpu.roll(y, rot, 1))
    return y * cos + swapped * sin_signed


def _lat_prep_kernel(z_ref, gqa_ref, gkva_ref, gkr_ref, cos_ref, sin_ref,
                     qa_ref, c32_ref, c16_ref, kr32_ref, kr16_ref, *, q_lora, kv_lora, rope_dim):
    z = z_ref[...]
    qa_ref[...] = _rms(z[:, :q_lora], gqa_ref[...]).astype(qa_ref.dtype)
    c = _rms(z[:, q_lora:q_lora + kv_lora], gkva_ref[...])
    c32_ref[...] = c
    c16_ref[...] = c.astype(c16_ref.dtype)
    r = z[:, q_lora + kv_lora:q_lora + kv_lora + LANES]
    y = r * lax.rsqrt(jnp.sum(r * r, axis=-1, keepdims=True) * (1.0 / rope_dim) + EPS) * gkr_ref[...]
    kr = _rope_padded(y, cos_ref[...], sin_ref[...], rope_dim // 2)
    kr32_ref[...] = kr[:, :rope_dim]
    kr16_ref[...] = kr.astype(kr16_ref.dtype)


def _lat_prep(z_lat, g_qa, g_kva, g_kr_pad, cos, sin, seq, q_lora, kv_lora, rope_dim):
    m, n = z_lat.shape
    tm = _tile(seq, 256)
    nb = seq // tm
    row = lambda i: (i, 0)
    const = lambda i: (0, 0)
    pos = lambda i: (i % nb, 0)
    return pl.pallas_call(
        functools.partial(_lat_prep_kernel, q_lora=q_lora, kv_lora=kv_lora, rope_dim=rope_dim),
        grid=(m // tm,),
        in_specs=[pl.BlockSpec((tm, n), row),
                  pl.BlockSpec((1, q_lora), const),
                  pl.BlockSpec((1, kv_lora), const),
                  pl.BlockSpec((1, LANES), const),
                  pl.BlockSpec((tm, LANES), pos),
                  pl.BlockSpec((tm, LANES), pos)],
        out_specs=[pl.BlockSpec((tm, q_lora), row),
                   pl.BlockSpec((tm, kv_lora), row),
                   pl.BlockSpec((tm, kv_lora), row),
                   pl.BlockSpec((tm, rope_dim), row),
                   pl.BlockSpec((tm, LANES), row)],
        out_shape=[jax.ShapeDtypeStruct((m, q_lora), BF16),
                   jax.ShapeDtypeStruct((m, kv_lora), F32),
                   jax.ShapeDtypeStruct((m, kv_lora), BF16),
                   jax.ShapeDtypeStruct((m, rope_dim), F32),
                   jax.ShapeDtypeStruct((m, LANES), BF16)],
        compiler_params=_params("parallel"),
        name="lat_prep",
    )(z_lat, g_qa.reshape(1, -1), g_kva.reshape(1, -1), g_kr_pad, cos, sin)


def _head_matmul_kernel(x_ref, w_ref, g_ref, *rest, rope_dim, scale):
    o_ref = rest[-1]
    groups = [slice(lo, lo + LANES) for lo in range(0, o_ref.shape[1], LANES)]
    for r0 in range(0, o_ref.shape[0], HEAD_ROWS):
        rows = slice(r0, min(r0 + HEAD_ROWS, o_ref.shape[0]))
        acc = jnp.dot(x_ref[rows, :], w_ref[...], preferred_element_type=F32)
        for g in groups:
            a = acc[:, g]
            if rope_dim is None:
                y = _rms(a, g_ref[...])
            else:
                cos_ref, sin_ref = rest[:2]
                y = a * lax.rsqrt(jnp.sum(a * a, axis=-1, keepdims=True) * (1.0 / rope_dim) + EPS) * g_ref[...]
                y = _rope_padded(y, cos_ref[rows, :], sin_ref[rows, :], rope_dim // 2)
            o_ref[rows, g] = (y * scale).astype(o_ref.dtype)


def _head_matmul(x, w, col0, n, g, scale=1.0, seq=None, cos=None, sin=None, rope_dim=None):
    m, k = x.shape
    tm = _tile(m, SHALLOW_ROWS // 2)
    tn = _tile(n, 512, LANES)
    assert col0 % tn == 0
    const = lambda i, j: (0, 0)
    in_specs = [pl.BlockSpec((tm, k), lambda i, j: (i, 0)),
                pl.BlockSpec((k, tn), lambda i, j: (0, col0 // tn + j)),
                pl.BlockSpec((1, LANES), const)]
    args = [x, w, g.reshape(1, LANES)]
    if rope_dim is not None:
        if tm >= seq:
            assert tm % seq == 0
            cos, sin = jnp.tile(cos, (tm // seq, 1)), jnp.tile(sin, (tm // seq, 1))
            nb = 1
        else:
            assert seq % tm == 0
            nb = seq // tm
        pos = lambda i, j: (i % nb, 0)
        in_specs += [pl.BlockSpec((tm, LANES), pos), pl.BlockSpec((tm, LANES), pos)]
        args += [cos, sin]
    return pl.pallas_call(
        functools.partial(_head_matmul_kernel, rope_dim=rope_dim, scale=scale),
        grid=(m // tm, n // tn),
        in_specs=in_specs,
        out_specs=pl.BlockSpec((tm, tn), lambda i, j: (i, j)),
        out_shape=jax.ShapeDtypeStruct((m, n), BF16),
        compiler_params=_params("parallel", "parallel"),
        name="head_matmul",
    )(*args)


def _attention_kernel(qn_ref, qr_ref, kn_ref, kr_ref, v_ref, o_ref,
                      s_scr, p_scr, m_scr, l_scr, a_scr, acc_scr, *, tk, causal):
    tq = qn_ref.shape[0]
    heads = [slice(g * LANES, (g + 1) * LANES) for g in range(qn_ref.shape[1] // LANES)]
    q = [jnp.concatenate([qn_ref[:, g], qr_ref[:, g]], axis=-1) for g in heads]
    chunks = [slice(r, min(r + SOFTMAX_ROWS, tq)) for r in range(0, tq, SOFTMAX_ROWS)]
    m_scr[...] = jnp.full(m_scr.shape, -1e30, F32)
    l_scr[...] = jnp.zeros(l_scr.shape, F32)
    acc_scr[...] = jnp.zeros(acc_scr.shape, F32)

    def key_rows(j):
        return pl.ds(pl.multiple_of(j * tk, tk), tk)

    def scores(j, slot, row0=0):
        rows = key_rows(j)
        kr = kr_ref[rows, :]
        for n, g in enumerate(heads):
            kb = jnp.concatenate([kn_ref[rows, g], kr], axis=-1)
            s_scr[slot, n, row0:, :] = lax.dot_general(q[n][row0:], kb, (((1,), (1,)), ((), ())),
                                                       preferred_element_type=F32)

    def softmax_pv(j, slot, row0=0, key0=None):
        rows = key_rows(j)
        live = [r for r in chunks if r.start >= row0]
        for n, g in enumerate(heads):
            for r in live:
                s = s_scr[slot, n, r, :]
                if key0 is not None:
                    qc = (r.start + lax.broadcasted_iota(jnp.int32, s.shape, 0)) // CHUNK
                    kc = (key0 + lax.broadcasted_iota(jnp.int32, s.shape, 1)) // CHUNK
                    s = jnp.where(kc <= qc, s, -1e30)
                m_old = m_scr[n, r, :]
                m_new = jnp.maximum(m_old, jnp.max(s, axis=-1, keepdims=True))
                alpha = jnp.exp2(m_old - m_new)
                if tk % LANES == 0:
                    p = jnp.exp2(s - jnp.concatenate([m_new] * (tk // LANES), axis=1))
                else:
                    p = jnp.exp2(s - m_new[:, :1])
                p_scr[slot, n, r, :] = p.astype(p_scr.dtype)
                m_scr[n, r, :] = m_new
                l_scr[n, r, :] = alpha * l_scr[n, r, :] + jnp.sum(p, axis=-1, keepdims=True)
                a_scr[slot, n, r, :] = alpha
            pv = jnp.dot(p_scr[slot, n, row0:, :], v_ref[rows, g], preferred_element_type=F32)
            for r in live:
                acc_scr[n, r, :] = a_scr[slot, n, r, :] * acc_scr[n, r, :] + pv[r.start - row0:r.stop - row0]

    slots = s_scr.shape[0]

    def body(jj, carry):
        for u in range(slots):
            scores(jj * slots + u, u)
        for u in range(slots):
            softmax_pv(jj * slots + u, u)
        return carry

    if causal:
        lax.fori_loop(0, pl.program_id(2), body, 0)
        first_diag = pl.program_id(2) * slots
        for d in range(slots):
            scores(first_diag + d, d, row0=d * tk)
        for d in range(slots):
            softmax_pv(first_diag + d, d, row0=d * tk, key0=d * tk)
    else:
        lax.fori_loop(0, kn_ref.shape[0] // (tk * slots), body, 0)
    for n, g in enumerate(heads):
        o_ref[:, g] = (acc_scr[n] / l_scr[n]).astype(o_ref.dtype)


def _attention(qn, qr, kn, kr, v, batch, tq_len, tk_len, heads, causal):
    if causal:
        tk = _tile(tk_len, 512)
        tq = _tile(tq_len, 1024, tk)
        assert tq_len == tk_len and tk % CHUNK == 0 and tq % tk == 0
    else:
        tq, tk = tq_len, tk_len
    nq = tq_len // tq
    slots = tq // tk if causal else 1
    assert heads % HEAD_GROUP == 0
    hg = heads // HEAD_GROUP
    wg = HEAD_GROUP * LANES
    return pl.pallas_call(
        functools.partial(_attention_kernel, tk=tk, causal=causal),
        grid=(batch, hg, nq),
        in_specs=[pl.BlockSpec((tq, wg), lambda b, h, i: (b * nq + i, h)),
                  pl.BlockSpec((tq, wg), lambda b, h, i: (b * nq + i, h)),
                  pl.BlockSpec((tk_len, wg), lambda b, h, i: (b, h)),
                  pl.BlockSpec((tk_len, LANES), lambda b, h, i: (b, 0)),
                  pl.BlockSpec((tk_len, wg), lambda b, h, i: (b, h))],
        out_specs=pl.BlockSpec((tq, wg), lambda b, h, i: (b * nq + i, h)),
        out_shape=jax.ShapeDtypeStruct((batch * tq_len, heads * LANES), BF16),
        scratch_shapes=[pltpu.VMEM((slots, HEAD_GROUP, tq, tk), F32),
                        pltpu.VMEM((slots, HEAD_GROUP, tq, tk), BF16),
                        pltpu.VMEM((HEAD_GROUP, tq, LANES), F32),
                        pltpu.VMEM((HEAD_GROUP, tq, LANES), F32),
                        pltpu.VMEM((slots, HEAD_GROUP, tq, LANES), F32),
                        pltpu.VMEM((HEAD_GROUP, tq, LANES), F32)],
        compiler_params=_params("parallel", "parallel", "arbitrary"),
        name="attention",
    )(qn, qr, kn, kr, v)


def _rope_tables_split(pos, dim):
    inv = ROPE_BASE ** (-jnp.arange(0, dim, 2, dtype=F32) / dim)
    ang = pos.astype(F32)[:, None] * inv[None, :]
    return jnp.cos(ang), jnp.sin(ang)


def _rope_tables_padded(pos, dim):
    cos, sin = _rope_tables_split(pos, dim)
    pad = jnp.zeros((pos.shape[0], LANES - dim), F32)
    return jnp.concatenate([cos, cos, pad], axis=1), jnp.concatenate([-sin, sin, pad], axis=1)


def _pad_lanes(g):
    return jnp.concatenate([g, jnp.zeros((LANES - g.shape[0],), g.dtype)]).reshape(1, LANES)


def _prepare_weights(lw, dims):
    ret_w, q_lora, kv_lora, rope_dim, heads, nope, vdim = dims
    w_in = lw['w_in']
    w_uq = lw['w_uq'].reshape(q_lora, heads, nope + rope_dim)
    w_uq_r = jnp.pad(w_uq[:, :, nope:], ((0, 0), (0, 0), (0, LANES - rope_dim)))
    w_ukv = lw['w_ukv'].reshape(kv_lora, heads, nope + vdim)
    return dict(
        w_lat=w_in[:, 4 * ret_w:].astype(BF16),
        w_uq=jnp.concatenate([w_uq[:, :, :nope].reshape(q_lora, heads * nope),
                              w_uq_r.reshape(q_lora, heads * LANES)], axis=1).astype(BF16),
        w_ukv=jnp.concatenate([w_ukv[:, :, :nope].reshape(kv_lora, heads * nope),
                               w_ukv[:, :, nope:].reshape(kv_lora, heads * vdim)], axis=1).astype(BF16),
        w_out=lw['w_out'].astype(BF16),
        g_qr=_pad_lanes(lw['g_qr']), g_kr=_pad_lanes(lw['g_kr']),
    )


def _layer(x, lw, pw, big_w, dims, batch, seq, pos, ret_chunk, state=None, cache_c=None, cache_kr=None):
    ret_w, q_lora, kv_lora, rope_dim, heads, nope, vdim = dims
    ret_heads, dk = lw['ret_heads'], lw['ret_dk']
    q_scale = (nope + rope_dim) ** -0.5 * LOG2_E
    cast = big_w['w1'][0].dtype == F32

    h1 = _ffn(x, lw['g_ffn1'], *big_w['w1'])
    h1, w1_bf = h1 if cast else (h1, big_w['w1'])
    xn = _rmsnorm(h1, lw['g_mix'])
    z_main = _matmul([(xn, big_w['w_in'], 0)], BF16, n=4 * ret_w)
    z_main, w_in_bf = z_main if cast else (z_main, big_w['w_in'])
    n_lat = q_lora + kv_lora + LANES
    z_lat = _matmul([(xn, pw['w_lat'], 0)], F32, n=n_lat + (-n_lat) % LAT_TN,
                    tm_pref=SHALLOW_ROWS // 2, tn_pref=LAT_TN)

    cos_r, sin_r = _rope_tables_split(pos, dk)
    log_g = jnp.log1p(-jnp.exp2(-5.0 - jnp.arange(ret_heads, dtype=F32)))
    ret_o, s_new = _retention(z_main, lw['g_ret'], log_g, cos_r, sin_r, batch, seq, ret_heads, dk,
                              ret_chunk, state)

    cos_m, sin_m = _rope_tables_padded(pos, rope_dim)
    qa, c_new, c_bf, kr_new, kr_bf = _lat_prep(z_lat, lw['g_qa'], lw['g_kva'], pw['g_kr'], cos_m, sin_m,
                                                seq, q_lora, kv_lora, rope_dim)
    hw = heads * LANES
    qn = _head_matmul(qa, pw['w_uq'], 0, hw, lw['g_qn'], q_scale)
    qr = _head_matmul(qa, pw['w_uq'], hw, hw, pw['g_qr'], q_scale, seq, cos_m, sin_m, rope_dim)
    if cache_c is None:
        keys, c_all, kr_all = seq, c_bf, kr_bf
    else:
        keys = cache_c.shape[1] + seq
        c_all = jnp.concatenate([cache_c.astype(BF16), c_bf.reshape(batch, seq, kv_lora)],
                                axis=1).reshape(batch * keys, kv_lora)
        kr_pad = jnp.pad(cache_kr, ((0, 0), (0, 0), (0, LANES - rope_dim))).astype(BF16)
        kr_all = jnp.concatenate([kr_pad, kr_bf.reshape(batch, seq, LANES)], axis=1).reshape(batch * keys, LANES)
    kn = _head_matmul(c_all, pw['w_ukv'], 0, hw, lw['g_kn'])
    v = _matmul([(c_all, pw['w_ukv'], 0)], BF16, n=hw, col0=hw, tm_pref=SHALLOW_ROWS)
    mla_o = _attention(qn, qr, kn, kr_all, v, batch, seq, keys, heads, causal=cache_c is None)

    h2 = _matmul([(ret_o, pw['w_out'], 0), (mla_o, pw['w_out'], ret_w)], F32, res=h1)
    y = _ffn(h2, lw['g_ffn2'], *big_w['w2'], g_final=lw['g_final'])
    y, w2_bf = y if cast else (y, big_w['w2'])
    return y, s_new, c_new, kr_new, dict(w1=w1_bf, w2=w2_bf, w_in=w_in_bf)


def kernel(x_prompt, x_sample, state_ret, cache_ckv, cache_krope,
           g_ffn1, w1_gate, w1_up, w1_down, g_mix, w_in, g_ret,
           g_qa, w_uq, g_qn, g_qr, g_kva, g_kr, w_ukv, g_kn, w_out,
           g_ffn2, w2_gate, w2_up, w2_down, g_final):
    batch, seq, d_model = x_prompt.shape
    dec_batch, dec_seq, _ = x_sample.shape
    depth, _, ret_heads, dk, dv = state_ret.shape
    past_len, kv_lora = cache_ckv.shape[2], cache_ckv.shape[3]
    rope_dim = cache_krope.shape[3]
    q_lora = g_qa.shape[1]
    nope = g_qn.shape[1]
    heads = w_uq.shape[2] // (nope + rope_dim)
    vdim = w_ukv.shape[2] // heads - nope
    ret_w = ret_heads * dk
    assert dk == dv and nope == LANES and vdim == LANES and rope_dim <= LANES // 2
    assert w_in.shape[2] == 4 * ret_w + q_lora + kv_lora + rope_dim
    dims = (ret_w, q_lora, kv_lora, rope_dim, heads, nope, vdim)

    pos_p = jnp.arange(seq)
    pos_s = past_len + jnp.arange(dec_seq)
    xp = x_prompt.reshape(batch * seq, d_model)
    xs = x_sample.reshape(dec_batch * dec_seq, d_model)
    outs = [[] for _ in range(6)]
    for l in range(depth):
        lw = dict(g_ffn1=g_ffn1[l], w1_gate=w1_gate[l], w1_up=w1_up[l], w1_down=w1_down[l],
                  g_mix=g_mix[l], w_in=w_in[l], g_ret=g_ret[l], g_qa=g_qa[l], w_uq=w_uq[l],
                  g_qn=g_qn[l], g_qr=g_qr[l], g_kva=g_kva[l], g_kr=g_kr[l], w_ukv=w_ukv[l],
                  g_kn=g_kn[l], w_out=w_out[l], g_ffn2=g_ffn2[l], w2_gate=w2_gate[l],
                  w2_up=w2_up[l], w2_down=w2_down[l], g_final=g_final[l],
                  ret_heads=ret_heads, ret_dk=dk)
        pw = _prepare_weights(lw, dims)
        big_w = dict(w1=(w1_gate[l], w1_up[l], w1_down[l]), w2=(w2_gate[l], w2_up[l], w2_down[l]),
                     w_in=w_in[l])
        if xs.shape[0] > FFN_CAST_ROWS:
            big_w = jax.tree.map(lambda w: w.astype(BF16), big_w)
        xs, s_s, c_s, k_s, big_w = _layer(xs, lw, pw, big_w, dims, dec_batch, dec_seq, pos_s, dec_seq,
                                          state_ret[l], cache_ckv[l], cache_krope[l])
        xp, s_p, c_p, k_p, _ = _layer(xp, lw, pw, big_w, dims, batch, seq, pos_p, _tile(seq, 256))
        outs[0].append(s_p)
        outs[1].append(c_p.reshape(batch, seq, kv_lora))
        outs[2].append(k_p.reshape(batch, seq, rope_dim))
        outs[3].append(s_s)
        outs[4].append(c_s.reshape(dec_batch, dec_seq, kv_lora))
        outs[5].append(k_s.reshape(dec_batch, dec_seq, rope_dim))
    return (xp.reshape(batch, seq, d_model), xs.reshape(dec_batch, dec_seq, d_model),
            *[jnp.stack(o) for o in outs])
```

```python
import functools

import jax
import jax.numpy as jnp
from jax import lax
from jax.experimental import pallas as pl
from jax.experimental.pallas import tpu as pltpu

F32 = jnp.float32
BF16 = jnp.bfloat16

EPS = 1e-6
ROPE_BASE = 10000.0
CHUNK = 64

LANES = 128
VMEM_LIMIT_BYTES = 56 * 1024 * 1024
WIDE_TN = 1024
LAT_TN = 256
LOG2_E = 1.4426950408889634
HEAD_GROUP = 2
RET_GROUP = 2
HEAD_ROWS = 256
SOFTMAX_ROWS = 32
FFN_ROWS = 512
SHALLOW_ROWS = 4096
FFN_CAST_ROWS = 512
ROW_CHUNK = 128


def _params(*sem):
    return pltpu.CompilerParams(dimension_semantics=sem, vmem_limit_bytes=VMEM_LIMIT_BYTES)


def _tile(n, pref, mult=8):
    if n <= pref:
        return n
    t = (pref // mult) * mult
    while t >= mult:
        if n % t == 0:
            return t
        t -= mult
    return n


def _rms(x, g):
    return x * lax.rsqrt(jnp.mean(x * x, axis=-1, keepdims=True) + EPS) * g


def _rmsnorm_kernel(x_ref, g_ref, o_ref):
    o_ref[...] = _rms(x_ref[...], g_ref[...]).astype(o_ref.dtype)


def _rmsnorm(x, g):
    m, d = x.shape
    tm = _tile(m, 256)
    return pl.pallas_call(
        _rmsnorm_kernel,
        grid=(m // tm,),
        in_specs=[pl.BlockSpec((tm, d), lambda i: (i, 0)),
                  pl.BlockSpec((1, d), lambda i: (0, 0))],
        out_specs=pl.BlockSpec((tm, d), lambda i: (i, 0)),
        out_shape=jax.ShapeDtypeStruct((m, d), BF16),
        compiler_params=_params("parallel"),
        name="rmsnorm",
    )(x, g.reshape(1, d))


def _ffn_kernel(x_hbm, g_ref, wg_ref, wu_ref, wd_ref, gf_ref, o_ref, *rest, final_norm, tn, cast):
    if cast:
        wg_out, wu_out, wd_out, xn_ref, x_sems = rest
        wg_out[...] = wg_ref[...].astype(BF16)
        wu_out[...] = wu_ref[...].astype(BF16)
        wd_out[...] = wd_ref[...].astype(BF16)
        wg_ref, wu_ref, wd_ref = wg_out, wu_out, wd_out
    else:
        xn_ref, x_sems = rest
    i = pl.program_id(0)
    j = pl.program_id(1)
    tm, d = o_ref.shape
    row_chunks = [(r, min(ROW_CHUNK, tm - r)) for r in range(0, tm, ROW_CHUNK)]

    def x_copy(k):
        r, n = row_chunks[k]
        return pltpu.make_async_copy(x_hbm.at[pl.ds(i * tm + r, n), :], o_ref.at[pl.ds(r, n), :], x_sems.at[k])

    @pl.when(j == 0)
    def _():
        for k in range(len(row_chunks)):
            x_copy(k).start()
        for k, (r, n) in enumerate(row_chunks):
            x_copy(k).wait()
            xn_ref[r:r + n, :] = _rms(o_ref[r:r + n, :], g_ref[...]).astype(BF16)

    for r0 in range(0, tm, FFN_ROWS):
        rows = slice(r0, min(r0 + FFN_ROWS, tm))
        xn = xn_ref[rows, :]
        a = jnp.dot(xn, wg_ref[...], preferred_element_type=F32)
        b = jnp.dot(xn, wu_ref[...], preferred_element_type=F32)
        h = (0.5 * (a * (1.0 / (1.0 + jnp.exp(-a)))) * b).astype(BF16)
        for n in range(0, d, tn):
            o_ref[rows, n:n + tn] += jnp.dot(h, wd_ref[:, n:n + tn], preferred_element_type=F32)

    if final_norm:
        @pl.when(j == pl.num_programs(1) - 1)
        def _():
            for r, n in row_chunks:
                o_ref[r:r + n, :] = _rms(o_ref[r:r + n, :], gf_ref[...])


def _ffn(x, g, wg, wu, wd, g_final=None):
    m, d = x.shape
    f = wg.shape[1]
    cast = wg.dtype == F32
    tm = _tile(m, FFN_CAST_ROWS if cast else 1024)
    tf = _tile(f, 256, LANES)
    final_norm = g_final is not None
    gf = (g_final if final_norm else g).reshape(1, d)
    w_specs = [pl.BlockSpec((d, tf), lambda i, j: (0, j)),
               pl.BlockSpec((d, tf), lambda i, j: (0, j)),
               pl.BlockSpec((tf, d), lambda i, j: (j, 0))]
    out_specs = pl.BlockSpec((tm, d), lambda i, j: (i, 0))
    out_shape = jax.ShapeDtypeStruct((m, d), F32)
    if cast:
        assert m == tm
        out_specs = [pl.BlockSpec((tm, d), lambda i, j: (i, 0), pipeline_mode=pl.Buffered(1))] + w_specs
        out_shape = [out_shape] + [jax.ShapeDtypeStruct(w.shape, BF16) for w in (wg, wu, wd)]
    out = pl.pallas_call(
        functools.partial(_ffn_kernel, final_norm=final_norm, tn=_tile(d, 512, LANES), cast=cast),
        grid=(m // tm, f // tf),
        in_specs=[pl.BlockSpec(memory_space=pl.ANY),
                  pl.BlockSpec((1, d), lambda i, j: (0, 0)),
                  *w_specs,
                  pl.BlockSpec((1, d), lambda i, j: (0, 0))],
        out_specs=out_specs,
        out_shape=out_shape,
        scratch_shapes=[pltpu.VMEM((tm, d), BF16),
                        pltpu.SemaphoreType.DMA((-(-tm // ROW_CHUNK),))],
        compiler_params=_params("parallel", "arbitrary"),
        name="ffn_cast" if cast else "ffn",
    )(x, g.reshape(1, d), wg, wu, wd, gf)
    return (out[0], tuple(out[1:])) if cast else out


def _matmul_kernel(*refs, n_pairs, has_res, n_valid):
    o_ref = refs[-1]
    acc = None
    for p in range(n_pairs):
        part = jnp.dot(refs[2 * p][...], refs[2 * p + 1][...], preferred_element_type=F32)
        acc = part if acc is None else acc + part
    if has_res:
        acc = refs[2 * n_pairs][...] + acc
    if n_valid is not None:
        col = pl.program_id(1) * o_ref.shape[1] + lax.broadcasted_iota(jnp.int32, acc.shape, 1)
        acc = jnp.where(col < n_valid, acc, 0.0)
    o_ref[...] = acc.astype(o_ref.dtype)


def _matmul(pairs, out_dtype, res=None, n=None, col0=0, tm_pref=1024, tn_pref=512):
    m = pairs[0][0].shape[0]
    n = pairs[0][1].shape[1] if n is None else n
    tm = _tile(m, tm_pref)
    tn = _tile(n, tn_pref, LANES)
    assert col0 % tn == 0
    w_cols = pairs[0][1].shape[1] - col0
    in_specs, args = [], []
    for x, w, row0 in pairs:
        k = x.shape[1]
        assert row0 % k == 0
        in_specs += [pl.BlockSpec((tm, k), lambda i, j: (i, 0)),
                     pl.BlockSpec((k, tn), lambda i, j, rb=row0 // k: (rb, col0 // tn + j))]
        args += [x, w]
    if res is not None:
        in_specs.append(pl.BlockSpec((tm, tn), lambda i, j: (i, j)))
        args.append(res)
    return pl.pallas_call(
        functools.partial(_matmul_kernel, n_pairs=len(pairs), has_res=res is not None,
                          n_valid=w_cols if w_cols < n else None),
        grid=(m // tm, n // tn),
        in_specs=in_specs,
        out_specs=pl.BlockSpec((tm, tn), lambda i, j: (i, j)),
        out_shape=jax.ShapeDtypeStruct((m, n), out_dtype),
        compiler_params=_params("parallel", "parallel"),
        name="matmul",
    )(*args)


def _retention_kernel(lg_ref, rq_ref, rk_ref, rv_ref, rg_ref, cos_ref, sin_ref, gain_ref, *rest,
                      has_state, k_scale):
    if has_state:
        s0_ref, o_ref, s_out_ref, s_scr, dmask_scr = rest
    else:
        o_ref, s_out_ref, s_scr, dmask_scr = rest
    c = pl.program_id(2)
    length = o_ref.shape[0]
    group = s_scr.shape[0]
    dk = s_scr.shape[1]
    ri = lax.broadcasted_iota(jnp.int32, (length, 1), 0).astype(F32)
    log_g = [lg_ref[pl.program_id(1) * group + n] for n in range(group)]

    @pl.when(c == 0)
    def _():
        diff = ri - lax.broadcasted_iota(jnp.int32, (1, length), 1).astype(F32)
        for n in range(group):
            if has_state:
                s_scr[n] = s0_ref[0, n]
            else:
                s_scr[n] = jnp.zeros((dk, dk), F32)
            dmask_scr[n] = jnp.where(diff >= 0, jnp.exp(log_g[n] * diff), 0.0)

    pos = pl.ds(pl.multiple_of(c * length, length), length)
    cos = cos_ref[pos, :]
    sin = sin_ref[pos, :]
    half = cos.shape[1]

    def rope(x):
        x = x.astype(F32)
        x1, x2 = x[:, :half], x[:, half:]
        return jnp.concatenate([x1 * cos - x2 * sin, x2 * cos + x1 * sin], axis=-1)

    for n in range(group):
        cols = slice(n * dk, (n + 1) * dk)
        lg = log_g[n]
        q = rope(rq_ref[:, cols])
        k = rope(rk_ref[:, cols]) * k_scale
        v = rv_ref[:, cols]
        inner = lax.dot_general(q.astype(BF16), k.astype(BF16), (((1,), (1,)), ((), ())),
                                preferred_element_type=F32) * dmask_scr[n]
        o = jnp.dot(inner.astype(BF16), v, preferred_element_type=F32)
        s = s_scr[n]
        q_dec = jnp.exp(lg * (ri + 1.0))
        o = o + jnp.dot((q * q_dec).astype(BF16), s.astype(BF16), preferred_element_type=F32)
        k_dec = jnp.exp(lg * (length - 1.0 - ri))
        s_new = jnp.exp(jnp.full((1, 1), lg * length, F32)) * s + lax.dot_general(
            (k * k_dec).astype(BF16), v, (((0,), (0,)), ((), ())), preferred_element_type=F32)
        s_scr[n] = s_new

        mu = jnp.mean(o, axis=-1, keepdims=True)
        oc = o - mu
        var = jnp.mean(oc * oc, axis=-1, keepdims=True)
        y = oc * lax.rsqrt(var + EPS) * gain_ref[:, cols]
        gate = rg_ref[:, cols].astype(F32)
        o_ref[:, cols] = ((gate * (1.0 / (1.0 + jnp.exp(-gate)))) * y).astype(o_ref.dtype)

    @pl.when(c == pl.num_programs(2) - 1)
    def _():
        s_out_ref[0] = s_scr[...]


def _retention(z, gain, log_g, cos, sin, batch, seq, heads, dk, chunk, state=None):
    assert dk == 2 * cos.shape[1] and dk % LANES == 0 and heads % RET_GROUP == 0
    nc = seq // chunk
    has_state = state is not None
    hg = heads // RET_GROUP
    wg = RET_GROUP * dk

    def strip(part):
        return pl.BlockSpec((chunk, wg), lambda b, h, c, lg: (b * nc + c, part * hg + h))

    in_specs = [strip(0), strip(1), strip(2), strip(3),
                pl.BlockSpec((seq, dk // 2), lambda b, h, c, lg: (0, 0)),
                pl.BlockSpec((seq, dk // 2), lambda b, h, c, lg: (0, 0)),
                pl.BlockSpec((1, wg), lambda b, h, c, lg: (0, h))]
    args = [z, z, z, z, cos, sin, gain.reshape(1, heads * dk)]
    if has_state:
        in_specs.append(pl.BlockSpec((1, RET_GROUP, dk, dk), lambda b, h, c, lg: (b, h, 0, 0)))
        args.append(state)
    return pl.pallas_call(
        functools.partial(_retention_kernel, has_state=has_state, k_scale=dk ** -0.5),
        grid_spec=pltpu.PrefetchScalarGridSpec(
            num_scalar_prefetch=1,
            grid=(batch, hg, nc),
            in_specs=in_specs,
            out_specs=[pl.BlockSpec((chunk, wg), lambda b, h, c, lg: (b * nc + c, h)),
                       pl.BlockSpec((1, RET_GROUP, dk, dk), lambda b, h, c, lg: (b, h, 0, 0))],
            scratch_shapes=[pltpu.VMEM((RET_GROUP, dk, dk), F32), pltpu.VMEM((RET_GROUP, chunk, chunk), F32)]),
        out_shape=[jax.ShapeDtypeStruct((batch * seq, heads * dk), BF16),
                   jax.ShapeDtypeStruct((batch, heads, dk, dk), F32)],
        compiler_params=_params("parallel", "parallel", "arbitrary"),
        name="retention",
    )(log_g, *args)


def _rope_padded(y, cos, sin_signed, rot):
    lane = lax.broadcasted_iota(jnp.int32, y.shape, 1)
    swapped = jnp.where(lane < rot, pltpu.roll(y, LANES - rot, 1), pltpu.roll(y, rot, 1))
    return y * cos + swapped * sin_signed


def _lat_prep_kernel(z_ref, gqa_ref, gkva_ref, gkr_ref, cos_ref, sin_ref,
                     qa_ref, c32_ref, c16_ref, kr32_ref, kr16_ref, *, q_lora, kv_lora, rope_dim):
    z = z_ref[...]
    qa_ref[...] = _rms(z[:, :q_lora], gqa_ref[...]).astype(qa_ref.dtype)
    c = _rms(z[:, q_lora:q_lora + kv_lora], gkva_ref[...])
    c32_ref[...] = c
    c16_ref[...] = c.astype(c16_ref.dtype)
    r = z[:, q_lora + kv_lora:q_lora + kv_lora + LANES]
    y = r * lax.rsqrt(jnp.sum(r * r, axis=-1, keepdims=True) * (1.0 / rope_dim) + EPS) * gkr_ref[...]
    kr = _rope_padded(y, cos_ref[...], sin_ref[...], rope_dim // 2)
    kr32_ref[...] = kr[:, :rope_dim]
    kr16_ref[...] = kr.astype(kr16_ref.dtype)


def _lat_prep(z_lat, g_qa, g_kva, g_kr_pad, cos, sin, seq, q_lora, kv_lora, rope_dim):
    m, n = z_lat.shape
    tm = _tile(seq, 256)
    nb = seq // tm
    row = lambda i: (i, 0)
    const = lambda i: (0, 0)
    pos = lambda i: (i % nb, 0)
    return pl.pallas_call(
        functools.partial(_lat_prep_kernel, q_lora=q_lora, kv_lora=kv_lora, rope_dim=rope_dim),
        grid=(m // tm,),
        in_specs=[pl.BlockSpec((tm, n), row),
                  pl.BlockSpec((1, q_lora), const),
                  pl.BlockSpec((1, kv_lora), const),
                  pl.BlockSpec((1, LANES), const),
                  pl.BlockSpec((tm, LANES), pos),
                  pl.BlockSpec((tm, LANES), pos)],
        out_specs=[pl.BlockSpec((tm, q_lora), row),
                   pl.BlockSpec((tm, kv_lora), row),
                   pl.BlockSpec((tm, kv_lora), row),
                   pl.BlockSpec((tm, rope_dim), row),
                   pl.BlockSpec((tm, LANES), row)],
        out_shape=[jax.ShapeDtypeStruct((m, q_lora), BF16),
                   jax.ShapeDtypeStruct((m, kv_lora), F32),
                   jax.ShapeDtypeStruct((m, kv_lora), BF16),
                   jax.ShapeDtypeStruct((m, rope_dim), F32),
                   jax.ShapeDtypeStruct((m, LANES), BF16)],
        compiler_params=_params("parallel"),
        name="lat_prep",
    )(z_lat, g_qa.reshape(1, -1), g_kva.reshape(1, -1), g_kr_pad, cos, sin)


def _head_matmul_kernel(x_ref, w_ref, g_ref, *rest, rope_dim, scale):
    o_ref = rest[-1]
    groups = [slice(lo, lo + LANES) for lo in range(0, o_ref.shape[1], LANES)]
    for r0 in range(0, o_ref.shape[0], HEAD_ROWS):
        rows = slice(r0, min(r0 + HEAD_ROWS, o_ref.shape[0]))
        acc = jnp.dot(x_ref[rows, :], w_ref[...], preferred_element_type=F32)
        for g in groups:
            a = acc[:, g]
            if rope_dim is None:
                y = _rms(a, g_ref[...])
            else:
                cos_ref, sin_ref = rest[:2]
                y = a * lax.rsqrt(jnp.sum(a * a, axis=-1, keepdims=True) * (1.0 / rope_dim) + EPS) * g_ref[...]
                y = _rope_padded(y, cos_ref[rows, :], sin_ref[rows, :], rope_dim // 2)
            o_ref[rows, g] = (y * scale).astype(o_ref.dtype)


def _head_matmul(x, w, col0, n, g, scale=1.0, seq=None, cos=None, sin=None, rope_dim=None):
    m, k = x.shape
    tm = _tile(m, SHALLOW_ROWS // 2)
    tn = _tile(n, 512, LANES)
    assert col0 % tn == 0
    const = lambda i, j: (0, 0)
    in_specs = [pl.BlockSpec((tm, k), lambda i, j: (i, 0)),
                pl.BlockSpec((k, tn), lambda i, j: (0, col0 // tn + j)),
                pl.BlockSpec((1, LANES), const)]
    args = [x, w, g.reshape(1, LANES)]
    if rope_dim is not None:
        if tm >= seq:
            assert tm % seq == 0
            cos, sin = jnp.tile(cos, (tm // seq, 1)), jnp.tile(sin, (tm // seq, 1))
            nb = 1
        else:
            assert seq % tm == 0
            nb = seq // tm
        pos = lambda i, j: (i % nb, 0)
        in_specs += [pl.BlockSpec((tm, LANES), pos), pl.BlockSpec((tm, LANES), pos)]
        args += [cos, sin]
    return pl.pallas_call(
        functools.partial(_head_matmul_kernel, rope_dim=rope_dim, scale=scale),
        grid=(m // tm, n // tn),
        in_specs=in_specs,
        out_specs=pl.BlockSpec((tm, tn), lambda i, j: (i, j)),
        out_shape=jax.ShapeDtypeStruct((m, n), BF16),
        compiler_params=_params("parallel", "parallel"),
        name="head_matmul",
    )(*args)


def _attention_kernel(qn_ref, qr_ref, kn_ref, kr_ref, v_ref, o_ref,
                      s_scr, p_scr, m_scr, l_scr, a_scr, acc_scr, *, tk, causal):
    tq = qn_ref.shape[0]
    heads = [slice(g * LANES, (g + 1) * LANES) for g in range(qn_ref.shape[1] // LANES)]
    q = [jnp.concatenate([qn_ref[:, g], qr_ref[:, g]], axis=-1) for g in heads]
    chunks = [slice(r, min(r + SOFTMAX_ROWS, tq)) for r in range(0, tq, SOFTMAX_ROWS)]
    m_scr[...] = jnp.full(m_scr.shape, -1e30, F32)
    l_scr[...] = jnp.zeros(l_scr.shape, F32)
    acc_scr[...] = jnp.zeros(acc_scr.shape, F32)

    def key_rows(j):
        return pl.ds(pl.multiple_of(j * tk, tk), tk)

    def scores(j, slot, row0=0):
        rows = key_rows(j)
        kr = kr_ref[rows, :]
        for n, g in enumerate(heads):
            kb = jnp.concatenate([kn_ref[rows, g], kr], axis=-1)
            s_scr[slot, n, row0:, :] = lax.dot_general(q[n][row0:], kb, (((1,), (1,)), ((), ())),
                                                       preferred_element_type=F32)

    def softmax_pv(j, slot, row0=0, key0=None):
        rows = key_rows(j)
        live = [r for r in chunks if r.start >= row0]
        for n, g in enumerate(heads):
            for r in live:
                s = s_scr[slot, n, r, :]
                if key0 is not None:
                    qc = (r.start + lax.broadcasted_iota(jnp.int32, s.shape, 0)) // CHUNK
                    kc = (key0 + lax.broadcasted_iota(jnp.int32, s.shape, 1)) // CHUNK
                    s = jnp.where(kc <= qc, s, -1e30)
                m_old = m_scr[n, r, :]
                m_new = jnp.maximum(m_old, jnp.max(s, axis=-1, keepdims=True))
                alpha = jnp.exp2(m_old - m_new)
                if tk % LANES == 0:
                    p = jnp.exp2(s - jnp.concatenate([m_new] * (tk // LANES), axis=1))
                else:
                    p = jnp.exp2(s - m_new[:, :1])
                p_scr[slot, n, r, :] = p.astype(p_scr.dtype)
                m_scr[n, r, :] = m_new
                l_scr[n, r, :] = alpha * l_scr[n, r, :] + jnp.sum(p, axis=-1, keepdims=True)
                a_scr[slot, n, r, :] = alpha
            pv = jnp.dot(p_scr[slot, n, row0:, :], v_ref[rows, g], preferred_element_type=F32)
            for r in live:
                acc_scr[n, r, :] = a_scr[slot, n, r, :] * acc_scr[n, r, :] + pv[r.start - row0:r.stop - row0]

    slots = s_scr.shape[0]

    def body(jj, carry):
        for u in range(slots):
            scores(jj * slots + u, u)
        for u in range(slots):
            softmax_pv(jj * slots + u, u)
        return carry

    if causal:
        lax.fori_loop(0, pl.program_id(2), body, 0)
        first_diag = pl.program_id(2) * slots
        for d in range(slots):
            scores(first_diag + d, d, row0=d * tk)
        for d in range(slots):
            softmax_pv(first_diag + d, d, row0=d * tk, key0=d * tk)
    else:
        lax.fori_loop(0, kn_ref.shape[0] // (tk * slots), body, 0)
    for n, g in enumerate(heads):
        o_ref[:, g] = (acc_scr[n] / l_scr[n]).astype(o_ref.dtype)


def _attention(qn, qr, kn, kr, v, batch, tq_len, tk_len, heads, causal):
    if causal:
        tk = _tile(tk_len, 512)
        tq = _tile(tq_len, 1024, tk)
        assert tq_len == tk_len and tk % CHUNK == 0 and tq % tk == 0
    else:
        tq, tk = tq_len, tk_len
    nq = tq_len // tq
    slots = tq // tk if causal else 1
    assert heads % HEAD_GROUP == 0
    hg = heads // HEAD_GROUP
    wg = HEAD_GROUP * LANES
    return pl.pallas_call(
        functools.partial(_attention_kernel, tk=tk, causal=causal),
        grid=(batch, hg, nq),
        in_specs=[pl.BlockSpec((tq, wg), lambda b, h, i: (b * nq + i, h)),
                  pl.BlockSpec((tq, wg), lambda b, h, i: (b * nq + i, h)),
                  pl.BlockSpec((tk_len, wg), lambda b, h, i: (b, h)),
                  pl.BlockSpec((tk_len, LANES), lambda b, h, i: (b, 0)),
                  pl.BlockSpec((tk_len, wg), lambda b, h, i: (b, h))],
        out_specs=pl.BlockSpec((tq, wg), lambda b, h, i: (b * nq + i, h)),
        out_shape=jax.ShapeDtypeStruct((batch * tq_len, heads * LANES), BF16),
        scratch_shapes=[pltpu.VMEM((slots, HEAD_GROUP, tq, tk), F32),
                        pltpu.VMEM((slots, HEAD_GROUP, tq, tk), BF16),
                        pltpu.VMEM((HEAD_GROUP, tq, LANES), F32),
                        pltpu.VMEM((HEAD_GROUP, tq, LANES), F32),
                        pltpu.VMEM((slots, HEAD_GROUP, tq, LANES), F32),
                        pltpu.VMEM((HEAD_GROUP, tq, LANES), F32)],
        compiler_params=_params("parallel", "parallel", "arbitrary"),
        name="attention",
    )(qn, qr, kn, kr, v)


def _rope_tables_split(pos, dim):
    inv = ROPE_BASE ** (-jnp.arange(0, dim, 2, dtype=F32) / dim)
    ang = pos.astype(F32)[:, None] * inv[None, :]
    return jnp.cos(ang), jnp.sin(ang)


def _rope_tables_padded(pos, dim):
    cos, sin = _rope_tables_split(pos, dim)
    pad = jnp.zeros((pos.shape[0], LANES - dim), F32)
    return jnp.concatenate([cos, cos, pad], axis=1), jnp.concatenate([-sin, sin, pad], axis=1)


def _pad_lanes(g):
    return jnp.concatenate([g, jnp.zeros((LANES - g.shape[0],), g.dtype)]).reshape(1, LANES)


def _prepare_weights(lw, dims):
    ret_w, q_lora, kv_lora, rope_dim, heads, nope, vdim = dims
    w_in = lw['w_in']
    w_uq = lw['w_uq'].reshape(q_lora, heads, nope + rope_dim)
    w_uq_r = jnp.pad(w_uq[:, :, nope:], ((0, 0), (0, 0), (0, LANES - rope_dim)))
    w_ukv = lw['w_ukv'].reshape(kv_lora, heads, nope + vdim)
    return dict(
        w_in=w_in.astype(BF16),
        w_uq=jnp.concatenate([w_uq[:, :, :nope].reshape(q_lora, heads * nope),
                              w_uq_r.reshape(q_lora, heads * LANES)], axis=1).astype(BF16),
        w_ukv=jnp.concatenate([w_ukv[:, :, :nope].reshape(kv_lora, heads * nope),
                               w_ukv[:, :, nope:].reshape(kv_lora, heads * vdim)], axis=1).astype(BF16),
        w_out=lw['w_out'].astype(BF16),
        g_qr=_pad_lanes(lw['g_qr']), g_kr=_pad_lanes(lw['g_kr']),
    )


def _layer(x, lw, pw, big_w, dims, batch, seq, pos, ret_chunk, state=None, cache_c=None, cache_kr=None):
    ret_w, q_lora, kv_lora, rope_dim, heads, nope, vdim = dims
    ret_heads, dk = lw['ret_heads'], lw['ret_dk']
    q_scale = (nope + rope_dim) ** -0.5 * LOG2_E
    cast = big_w['w1'][0].dtype == F32

    h1 = _ffn(x, lw['g_ffn1'], *big_w['w1'])
    h1, w1_bf = h1 if cast else (h1, big_w['w1'])
    xn = _rmsnorm(h1, lw['g_mix'])
    z_main = _matmul([(xn, pw['w_in'], 0)], BF16, n=4 * ret_w, tn_pref=WIDE_TN)
    n_lat = q_lora + kv_lora + LANES
    z_lat = _matmul([(xn, pw['w_in'], 0)], F32, n=n_lat + (-n_lat) % LAT_TN, col0=4 * ret_w,
                    tm_pref=SHALLOW_ROWS // 2, tn_pref=LAT_TN)

    cos_r, sin_r = _rope_tables_split(pos, dk)
    log_g = jnp.log1p(-jnp.exp2(-5.0 - jnp.arange(ret_heads, dtype=F32)))
    ret_o, s_new = _retention(z_main, lw['g_ret'], log_g, cos_r, sin_r, batch, seq, ret_heads, dk,
                              ret_chunk, state)

    cos_m, sin_m = _rope_tables_padded(pos, rope_dim)
    qa, c_new, c_bf, kr_new, kr_bf = _lat_prep(z_lat, lw['g_qa'], lw['g_kva'], pw['g_kr'], cos_m, sin_m,
                                                seq, q_lora, kv_lora, rope_dim)
    hw = heads * LANES
    qn = _head_matmul(qa, pw['w_uq'], 0, hw, lw['g_qn'], q_scale)
    qr = _head_matmul(qa, pw['w_uq'], hw, hw, pw['g_qr'], q_scale, seq, cos_m, sin_m, rope_dim)
    if cache_c is None:
        keys, c_all, kr_all = seq, c_bf, kr_bf
    else:
        keys = cache_c.shape[1] + seq
        c_all = jnp.concatenate([cache_c.astype(BF16), c_bf.reshape(batch, seq, kv_lora)],
                                axis=1).reshape(batch * keys, kv_lora)
        kr_pad = jnp.pad(cache_kr, ((0, 0), (0, 0), (0, LANES - rope_dim))).astype(BF16)
        kr_all = jnp.concatenate([kr_pad, kr_bf.reshape(batch, seq, LANES)], axis=1).reshape(batch * keys, LANES)
    kn = _head_matmul(c_all, pw['w_ukv'], 0, hw, lw['g_kn'])
    v = _matmul([(c_all, pw['w_ukv'], 0)], BF16, n=hw, col0=hw, tm_pref=SHALLOW_ROWS)
    mla_o = _attention(qn, qr, kn, kr_all, v, batch, seq, keys, heads, causal=cache_c is None)

    h2 = _matmul([(ret_o, pw['w_out'], 0), (mla_o, pw['w_out'], ret_w)], F32, res=h1, tn_pref=WIDE_TN)
    y = _ffn(h2, lw['g_ffn2'], *big_w['w2'], g_final=lw['g_final'])
    y, w2_bf = y if cast else (y, big_w['w2'])
    return y, s_new, c_new, kr_new, dict(w1=w1_bf, w2=w2_bf)


def kernel(x_prompt, x_sample, state_ret, cache_ckv, cache_krope,
           g_ffn1, w1_gate, w1_up, w1_down, g_mix, w_in, g_ret,
           g_qa, w_uq, g_qn, g_qr, g_kva, g_kr, w_ukv, g_kn, w_out,
           g_ffn2, w2_gate, w2_up, w2_down, g_final):
    batch, seq, d_model = x_prompt.shape
    dec_batch, dec_seq, _ = x_sample.shape
    depth, _, ret_heads, dk, dv = state_ret.shape
    past_len, kv_lora = cache_ckv.shape[2], cache_ckv.shape[3]
    rope_dim = cache_krope.shape[3]
    q_lora = g_qa.shape[1]
    nope = g_qn.shape[1]
    heads = w_uq.shape[2] // (nope + rope_dim)
    vdim = w_ukv.shape[2] // heads - nope
    ret_w = ret_heads * dk
    assert dk == dv and nope == LANES and vdim == LANES and rope_dim <= LANES // 2
    assert w_in.shape[2] == 4 * ret_w + q_lora + kv_lora + rope_dim
    dims = (ret_w, q_lora, kv_lora, rope_dim, heads, nope, vdim)

    pos_p = jnp.arange(seq)
    pos_s = past_len + jnp.arange(dec_seq)
    xp = x_prompt.reshape(batch * seq, d_model)
    xs = x_sample.reshape(dec_batch * dec_seq, d_model)
    outs = [[] for _ in range(6)]
    for l in range(depth):
        lw = dict(g_ffn1=g_ffn1[l], w1_gate=w1_gate[l], w1_up=w1_up[l], w1_down=w1_down[l],
                  g_mix=g_mix[l], w_in=w_in[l], g_ret=g_ret[l], g_qa=g_qa[l], w_uq=w_uq[l],
                  g_qn=g_qn[l], g_qr=g_qr[l], g_kva=g_kva[l], g_kr=g_kr[l], w_ukv=w_ukv[l],
                  g_kn=g_kn[l], w_out=w_out[l], g_ffn2=g_ffn2[l], w2_gate=w2_gate[l],
                  w2_up=w2_up[l], w2_down=w2_down[l], g_final=g_final[l],
                  ret_heads=ret_heads, ret_dk=dk)
        pw = _prepare_weights(lw, dims)
        big_w = dict(w1=(w1_gate[l], w1_up[l], w1_down[l]), w2=(w2_gate[l], w2_up[l], w2_down[l]))
        if xs.shape[0] > FFN_CAST_ROWS:
            big_w = jax.tree.map(lambda w: w.astype(BF16), big_w)
        xs, s_s, c_s, k_s, big_w = _layer(xs, lw, pw, big_w, dims, dec_batch, dec_seq, pos_s, dec_seq,
                                          state_ret[l], cache_ckv[l], cache_krope[l])
        xp, s_p, c_p, k_p, _ = _layer(xp, lw, pw, big_w, dims, batch, seq, pos_p, _tile(seq, 256))
        outs[0].append(s_p)
        outs[1].append(c_p.reshape(batch, seq, kv_lora))
        outs[2].append(k_p.reshape(batch, seq, rope_dim))
        outs[3].append(s_s)
        outs[4].append(c_s.reshape(dec_batch, dec_seq, kv_lora))
        outs[5].append(k_s.reshape(dec_batch, dec_seq, rope_dim))
    return (xp.reshape(batch, seq, d_model), xs.reshape(dec_batch, dec_seq, d_model),
            *[jnp.stack(o) for o in outs])
```
